```python
import jax
import jax.numpy as jnp
from jax import lax
import numpy as np

D_MODEL = 2048
BATCH = 4
SEQ = 4096
DEPTH = 2

F32 = jnp.float32
CHUNK = 64
Q_BLOCK = 128
N_GROUPS = 4
GW = D_MODEL // N_GROUPS
ML_HEADS = 4
ML_DH = GW // ML_HEADS
ML_CONV = 4
RW_DH = 64
RW_HEADS = GW // RW_DH
RW_DECAY_LORA = max(32, int(round(1.8 * D_MODEL ** 0.5 / 32)) * 32)
RW_A_LORA = max(32, int(round(1.8 * D_MODEL ** 0.5 / 32)) * 32)
RW_GATE_LORA = max(32, int(round(0.6 * D_MODEL ** 0.8 / 32)) * 32)
FX_HEADS = 4
FX_DH = GW // FX_HEADS
HG_HEADS = 4
D_FF = ((8 * D_MODEL // 3 + 127) // 128) * 128
N_EXPERTS = 8
TOP_K = 2
N_DENSE = (DEPTH + 1) // 2
N_MOE = DEPTH // 2
ALPHA = (2 * DEPTH) ** 0.25
BETA = (8 * DEPTH) ** -0.25
LN_EPS = 1e-5
NORM_EPS = 1e-6
RW_GN_EPS = 64e-5
NEG_BIG = -1e30
EXP_CLIP = 60.0

ML_N = 4 * GW + 2 * ML_HEADS
RW_N = 3 * GW + RW_DECAY_LORA + RW_A_LORA + RW_GATE_LORA
FX_N = 3 * GW + FX_HEADS
HG_N = 4 * GW
N_IN = ML_N + RW_N + FX_N + HG_N

kernel_name = 'hybrid_mlstm_rwkv7_fox_hgrn2_deepnorm_moe'


def _split(a, sizes):
    return jnp.split(a, np.cumsum(sizes)[:-1].tolist(), axis=-1)


def shift_seq(x, n):
    return jnp.pad(x, ((0, 0), (n, 0), (0, 0)))[:, :x.shape[1]]


def heads(x, h):
    return x.reshape(x.shape[:-1] + (h, -1))


def to_chunks(x):
    b, s, h = x.shape[:3]
    x = x.reshape((b, s // CHUNK, CHUNK, h) + x.shape[3:])
    return jnp.moveaxis(x, (1, 3), (0, 2))


def from_chunks(y):
    y = jnp.moveaxis(y, (0, 2), (1, 3))
    return y.reshape(y.shape[0], -1, y.shape[3] * y.shape[4])


def layer_norm(x, g, b):
    xf = x.astype(F32)
    mu = xf.mean(-1, keepdims=True)
    var = jnp.square(xf - mu).mean(-1, keepdims=True)
    return ((xf - mu) * lax.rsqrt(var + LN_EPS) * g + b).astype(x.dtype)


def head_rms_norm(x, n_heads, g):
    xh = heads(x.astype(F32), n_heads)
    xh = xh * lax.rsqrt(jnp.mean(xh * xh, -1, keepdims=True) + NORM_EPS)
    return xh.reshape(x.shape) * g


def mlstm_mixer(p, conv_w, b_i, b_f, norm_g):
    q, k, v, o, ig, fg = _split(p, (GW, GW, GW, GW, ML_HEADS, ML_HEADS))
    qk = jnp.concatenate([q, k], -1)
    acc = conv_w[0] * qk
    for j in range(1, ML_CONV):
        acc = acc + conv_w[j] * shift_seq(qk, j)
    q, k = jnp.split(jax.nn.silu(acc), 2, axis=-1)
    q = heads(q, ML_HEADS).astype(F32)
    k = heads(k, ML_HEADS).astype(F32) * ML_DH ** -0.5
    v = heads(v, ML_HEADS).astype(F32)
    log_i = (ig + b_i).astype(F32)
    log_f = jax.nn.log_sigmoid((fg + b_f).astype(F32))
    bsz = p.shape[0]
    tri = jnp.tril(jnp.ones((CHUNK, CHUNK), bool))

    def step(carry, xs):
        c_mem, n_mem, m = carry
        qb, kb, vb, ib, fb = xs
        b = jnp.cumsum(fb, -1)
        b_last = b[..., -1]
        dmat = jnp.where(tri, b[..., :, None] - b[..., None, :] + ib[..., None, :], NEG_BIG)
        m_inter = m[..., None] + b
        m_t = jnp.maximum(m_inter, dmat.max(-1))
        att = jnp.einsum('bhtd,bhsd->bhts', qb, kb) * jnp.exp(dmat - m_t[..., None])
        s_inter = jnp.exp(m_inter - m_t)
        num = s_inter[..., None] * jnp.einsum('bhvd,bhtd->bhtv', c_mem, qb) + jnp.einsum('bhts,bhsv->bhtv', att, vb)
        den = s_inter * jnp.einsum('bhd,bhtd->bht', n_mem, qb) + att.sum(-1)
        h = num / jnp.maximum(jnp.abs(den), jnp.exp(-m_t))[..., None]
        d_end = b_last[..., None] - b + ib
        m_new = jnp.maximum(m + b_last, d_end.max(-1))
        w_end = jnp.exp(d_end - m_new[..., None])
        s_state = jnp.exp(m + b_last - m_new)
        c_mem = s_state[..., None, None] * c_mem + jnp.einsum('bhs,bhsv,bhsd->bhvd', w_end, vb, kb)
        n_mem = s_state[..., None] * n_mem + jnp.einsum('bhs,bhsd->bhd', w_end, kb)
        return (c_mem, n_mem, m_new), h

    init = (jnp.zeros((bsz, ML_HEADS, ML_DH, ML_DH), F32),
            jnp.zeros((bsz, ML_HEADS, ML_DH), F32),
            jnp.zeros((bsz, ML_HEADS), F32))
    _, h = lax.scan(step, init, (to_chunks(q), to_chunks(k), to_chunks(v), to_chunks(log_i), to_chunks(log_f)))
    h = head_rms_norm(from_chunks(h), ML_HEADS, norm_g)
    return (jax.nn.sigmoid(o.astype(F32)) * h).astype(p.dtype)


def rwkv7_mixer(p, mu, w0, w2, a0, a2, g2, k_k, k_a, r_k, gn_g, gn_b):
    p = p + (shift_seq(p, 1) - p) * mu
    r, k, v, wd, ad, gd = _split(p, (GW, GW, GW, RW_DECAY_LORA, RW_A_LORA, RW_GATE_LORA))
    w = -jax.nn.softplus(-(w0 + jnp.tanh(wd) @ w2)) - 0.5
    decay = jnp.exp(-jnp.exp(w.astype(F32)))
    a = jax.nn.sigmoid(a0 + ad @ a2)
    g = jax.nn.sigmoid(gd) @ g2
    kk = heads(k * k_k, RW_HEADS).astype(F32)
    kk = kk * lax.rsqrt(jnp.maximum(jnp.sum(kk * kk, -1, keepdims=True), 1e-24))
    k = k * (1 + (a - 1) * k_a)
    r_h, k_h, v_h, a_h, w_h = [heads(t, RW_HEADS).astype(F32) for t in (r, k, v, a, decay)]
    bsz = p.shape[0]

    def step(state, xs):
        rt, wt, kt, vt, kkt, at = xs
        sa = jnp.einsum('bhvk,bhk->bhv', state, kkt)
        state = state * wt[:, :, None, :] - sa[..., None] * (kkt * at)[:, :, None, :] + vt[..., None] * kt[:, :, None, :]
        return state, jnp.einsum('bhvk,bhk->bhv', state, rt)

    xs = tuple(jnp.moveaxis(t, 1, 0) for t in (r_h, w_h, k_h, v_h, kk, a_h))
    _, y = lax.scan(step, jnp.zeros((bsz, RW_HEADS, RW_DH, RW_DH), F32), xs)
    y = jnp.moveaxis(y, 0, 1)
    mu_y = y.mean(-1, keepdims=True)
    var_y = jnp.square(y - mu_y).mean(-1, keepdims=True)
    y = ((y - mu_y) * lax.rsqrt(var_y + RW_GN_EPS)).reshape(p.shape[0], p.shape[1], GW) * gn_g + gn_b
    bonus = (jnp.sum(r_h * k_h * r_k, -1, keepdims=True) * v_h).reshape(y.shape)
    return ((y + bonus) * g).astype(p.dtype)


def fox_mixer(p, b_f):
    q, k, v, fg = _split(p, (GW, GW, GW, FX_HEADS))
    bsz, s = p.shape[:2]
    q = jnp.moveaxis(heads(q, FX_HEADS), 2, 1).astype(F32) * FX_DH ** -0.5
    k = jnp.moveaxis(heads(k, FX_HEADS), 2, 1).astype(F32)
    v = jnp.moveaxis(heads(v, FX_HEADS), 2, 1).astype(F32)
    cum_f = jnp.moveaxis(jnp.cumsum(jax.nn.log_sigmoid((fg + b_f).astype(F32)), axis=1), 2, 1)
    nq = s // Q_BLOCK
    qb = jnp.moveaxis(q.reshape(bsz, FX_HEADS, nq, Q_BLOCK, FX_DH), 2, 0)
    fb = jnp.moveaxis(cum_f.reshape(bsz, FX_HEADS, nq, Q_BLOCK), 2, 0)
    qpos = jnp.arange(s).reshape(nq, Q_BLOCK)
    kpos = jnp.arange(s)

    def block(xs):
        qi, fi, ti = xs
        logits = jnp.einsum('bhtd,bhsd->bhts', qi, k) + fi[..., None] - cum_f[:, :, None, :]
        logits = jnp.where(kpos[None, :] <= ti[:, None], logits, NEG_BIG)
        return jnp.einsum('bhts,bhsd->bhtd', jax.nn.softmax(logits, axis=-1), v)

    o = lax.map(block, (qb, fb, qpos))
    o = jnp.moveaxis(o, 0, 2).reshape(bsz, FX_HEADS, s, FX_DH)
    return jnp.moveaxis(o, 1, 2).reshape(bsz, s, GW).astype(p.dtype)


def hgrn2_mixer(p, lb, norm_g):
    q, z, i, g = _split(p, (GW, GW, GW, GW))
    q = heads(jax.nn.silu(q), HG_HEADS).astype(F32)
    i = heads(jax.nn.silu(i), HG_HEADS).astype(F32)
    z = heads(z.astype(F32), HG_HEADS)
    lbh = heads(lb.astype(F32), HG_HEADS)
    log_f = jax.nn.log_sigmoid(z) + jnp.log1p(lbh * jnp.exp(jnp.minimum(-z, EXP_CLIP)))
    k = (1 - lbh) * jax.nn.sigmoid(-z)
    bsz = p.shape[0]
    dk = q.shape[-1]
    dv = i.shape[-1]
    tri = jnp.tril(jnp.ones((CHUNK, CHUNK), bool))[..., None]

    def step(state, xs):
        qb, kb, vb, gb = xs
        b = jnp.cumsum(gb, axis=2)
        b_last = b[:, :, -1]
        o_inter = jnp.einsum('bhtc,bhcv->bhtv', qb * jnp.exp(b), state)
        decay = jnp.exp(jnp.where(tri, b[:, :, :, None, :] - b[:, :, None, :, :], NEG_BIG))
        att = jnp.einsum('bhtc,bhsc,bhtsc->bhts', qb, kb, decay)
        o = o_inter + jnp.einsum('bhts,bhsv->bhtv', att, vb)
        state = jnp.exp(b_last)[..., None] * state + jnp.einsum('bhsc,bhsv->bhcv', kb * jnp.exp(b_last[:, :, None] - b), vb)
        return state, o

    _, o = lax.scan(step, jnp.zeros((bsz, HG_HEADS, dk, dv), F32),
                    (to_chunks(q), to_chunks(k), to_chunks(i), to_chunks(log_f)))
    o = from_chunks(o) * jax.nn.sigmoid(g.astype(F32))
    return head_rms_norm(o, HG_HEADS, norm_g).astype(p.dtype)


def swiglu(x, wg, wu, wd):
    return (jax.nn.silu(x @ wg) * (x @ wu)) @ wd


def moe_swiglu(x, router, wg, wu, wd):
    probs = jax.nn.softmax((x @ router).astype(F32), axis=-1)
    top_p, top_i = lax.top_k(probs, TOP_K)
    top_p = top_p / top_p.sum(-1, keepdims=True)
    comb = jnp.sum(jax.nn.one_hot(top_i, N_EXPERTS, dtype=F32) * top_p[..., None], axis=-2).astype(x.dtype)
    y = comb[..., 0:1] * swiglu(x, wg[0], wu[0], wd[0])
    for e in range(1, N_EXPERTS):
        y = y + comb[..., e:e + 1] * swiglu(x, wg[e], wu[e], wd[e])
    return y


def setup_inputs(seed: int = 0) -> dict:
    key = jax.random.key(seed)
    ks = iter(jax.random.split(key, 40))

    def nrm(shape, scale):
        return scale * jax.random.normal(next(ks), shape, F32)

    L = DEPTH
    return {
        'x': nrm((BATCH, SEQ, D_MODEL), 1.0),
        'w_in': nrm((L, D_MODEL, N_IN), D_MODEL ** -0.5),
        'w_out': nrm((L, D_MODEL, D_MODEL), BETA * D_MODEL ** -0.5),
        'ln1_g': 1.0 + nrm((L, D_MODEL), 0.02),
        'ln1_b': nrm((L, D_MODEL), 0.02),
        'ln2_g': 1.0 + nrm((L, D_MODEL), 0.02),
        'ln2_b': nrm((L, D_MODEL), 0.02),
        'ml_conv': nrm((L, ML_CONV, 2 * GW), ML_CONV ** -0.5),
        'ml_b_i': nrm((L, ML_HEADS), 0.1),
        'ml_b_f': jnp.linspace(3.0, 6.0, ML_HEADS) + nrm((L, ML_HEADS), 0.1),
        'ml_norm_g': 1.0 + nrm((L, GW), 0.02),
        'rw_mu': jax.random.uniform(next(ks), (L, RW_N), F32),
        'rw_w0': jnp.linspace(-6.0, -1.0, GW) + nrm((L, GW), 0.1),
        'rw_w2': nrm((L, RW_DECAY_LORA, GW), 0.1 * RW_DECAY_LORA ** -0.5),
        'rw_a0': nrm((L, GW), 0.1),
        'rw_a2': nrm((L, RW_A_LORA, GW), RW_A_LORA ** -0.5),
        'rw_g2': nrm((L, RW_GATE_LORA, GW), RW_GATE_LORA ** -0.5),
        'rw_k_k': 0.85 + nrm((L, GW), 0.02),
        'rw_k_a': 1.0 + nrm((L, GW), 0.02),
        'rw_r_k': nrm((L, RW_HEADS, RW_DH), 0.1),
        'rw_gn_g': 1.0 + nrm((L, GW), 0.02),
        'rw_gn_b': nrm((L, GW), 0.02),
        'fx_b_f': jnp.linspace(1.0, 4.0, FX_HEADS) + nrm((L, FX_HEADS), 0.1),
        'hg_lb_logits': nrm((L, GW), 0.5),
        'hg_norm_g': 1.0 + nrm((L, GW), 0.02),
        'ffn_w_gate': nrm((N_DENSE, D_MODEL, D_FF), D_MODEL ** -0.5),
        'ffn_w_up': nrm((N_DENSE, D_MODEL, D_FF), D_MODEL ** -0.5),
        'ffn_w_down': nrm((N_DENSE, D_FF, D_MODEL), BETA * D_FF ** -0.5),
        'moe_router': nrm((N_MOE, D_MODEL, N_EXPERTS), D_MODEL ** -0.5),
        'moe_w_gate': nrm((N_MOE, N_EXPERTS, D_MODEL, D_FF), D_MODEL ** -0.5),
        'moe_w_up': nrm((N_MOE, N_EXPERTS, D_MODEL, D_FF), D_MODEL ** -0.5),
        'moe_w_down': nrm((N_MOE, N_EXPERTS, D_FF, D_MODEL), BETA * D_FF ** -0.5),
    }


def reference(x, w_in, w_out, ln1_g, ln1_b, ln2_g, ln2_b, ml_conv, ml_b_i, ml_b_f, ml_norm_g,
              rw_mu, rw_w0, rw_w2, rw_a0, rw_a2, rw_g2, rw_k_k, rw_k_a, rw_r_k, rw_gn_g, rw_gn_b,
              fx_b_f, hg_lb_logits, hg_norm_g, ffn_w_gate, ffn_w_up, ffn_w_down,
              moe_router, moe_w_gate, moe_w_up, moe_w_down):
    lb_probs = jax.nn.softmax(hg_lb_logits.astype(F32), axis=0)
    lower_bounds = jnp.cumsum(lb_probs, axis=0) - lb_probs[0]
    for l in range(DEPTH):
        p = x @ w_in[l]
        p_ml, p_rw, p_fx, p_hg = _split(p, (ML_N, RW_N, FX_N, HG_N))
        y_ml = mlstm_mixer(p_ml, ml_conv[l], ml_b_i[l], ml_b_f[l], ml_norm_g[l])
        y_rw = rwkv7_mixer(p_rw, rw_mu[l], rw_w0[l], rw_w2[l], rw_a0[l], rw_a2[l], rw_g2[l],
                           rw_k_k[l], rw_k_a[l], rw_r_k[l], rw_gn_g[l], rw_gn_b[l])
        y_fx = fox_mixer(p_fx, fx_b_f[l])
        y_hg = hgrn2_mixer(p_hg, lower_bounds[l], hg_norm_g[l])
        mix = jnp.concatenate([y_ml, y_rw, y_fx, y_hg], axis=-1) @ w_out[l]
        x = layer_norm(ALPHA * x + mix, ln1_g[l], ln1_b[l])
        if l % 2 == 0:
            j = l // 2
            ff = swiglu(x, ffn_w_gate[j], ffn_w_up[j], ffn_w_down[j])
        else:
            j = l // 2
            ff = moe_swiglu(x, moe_router[j], moe_w_gate[j], moe_w_up[j], moe_w_down[j])
        x = layer_norm(ALPHA * x + ff, ln2_g[l], ln2_b[l])
    return x
```

```python
import functools

import jax
import jax.numpy as jnp
import numpy as np
from jax import lax
from jax.experimental import pallas as pl
from jax.experimental.pallas import tpu as pltpu

F32 = jnp.float32
MXU_DTYPE = jnp.bfloat16
HIGHEST = lax.Precision.HIGHEST

D_MODEL = 2048
N_GROUPS = 4
GW = D_MODEL // N_GROUPS
ML_HEADS = 4
ML_DH = GW // ML_HEADS
ML_CONV = 4
RW_DH = 64
RW_HEADS = GW // RW_DH
RW_DECAY_LORA = 96
RW_A_LORA = 96
RW_GATE_LORA = 256
FX_HEADS = 4
FX_DH = GW // FX_HEADS
HG_HEADS = 4
D_FF = 5504
N_EXPERTS = 8
DEPTH = 2
ALPHA = (2 * DEPTH) ** 0.25
LN_EPS = 1e-5
NORM_EPS = 1e-6
RW_GN_EPS = 64e-5
NEG_BIG = -1e30
EXP_CLIP = 60.0

LANES = 128
RW_PAD = LANES
RW_W = RW_HEADS * RW_PAD
D_FF_PAD = 5632
FF_TILE = 512
ROW_TILE = 512
PROJ_COL_TILE = 256
FLASH_TILE = 512
PREP_TILE = 512
HG_CHUNK = 128
HG_LEVELS = 7
RW_CHUNK = 128
RW_LEVELS = 7
VMEM_LIMIT = 56 * 1024 * 1024

C_RW_R, C_RW_K, C_RW_V = 0, 1024, 2048
C_ML_Q, C_ML_K, C_ML_V, C_ML_O = 3072, 3584, 4096, 4608
C_FX_Q, C_FX_K, C_FX_V = 5120, 5632, 6144
C_HG_Q, C_HG_Z, C_HG_I, C_HG_G = 6656, 7168, 7680, 8192
C_RW_GD = 8704
C_ML_G = 8960
C_FX_G = 9088
C_RW_WD, C_RW_AD = 9216, 9344
N_PACK = 9472


def _params(n_grid):
    return pltpu.CompilerParams(dimension_semantics=("arbitrary",) * n_grid, vmem_limit_bytes=VMEM_LIMIT)


def _dot(a, b):
    return jnp.dot(a, b, preferred_element_type=F32)


def _dot_nt(a, b):
    return lax.dot_general(a, b, (((1,), (1,)), ((), ())), preferred_element_type=F32)


def _dot_tn(a, b):
    return lax.dot_general(a, b, (((0,), (0,)), ((), ())), preferred_element_type=F32)


def _dot_hi(a, b):
    return jnp.dot(a, b, precision=HIGHEST, preferred_element_type=F32)


def _mx(x):
    return x.astype(MXU_DTYPE)


def _split3(x):
    hi = _mx(x)
    r1 = x - hi.astype(F32)
    mid = _mx(r1)
    lo = _mx(r1 - mid.astype(F32))
    return hi, mid, lo


def _select_matmul(mat, x):
    c = x.shape[1]
    hi, mid, lo = _split3(x)
    r = _dot(mat, jnp.concatenate([hi, mid, lo], axis=1))
    return (r[:, :c] + r[:, c:2 * c]) + r[:, 2 * c:]


def _softplus(x):
    return jnp.maximum(x, 0.0) + jnp.log1p(jnp.exp(-jnp.abs(x)))


def _log_sigmoid(x):
    return -_softplus(-x)


def _sigmoid(x):
    return jax.nn.sigmoid(x)


def _silu(x):
    return x * _sigmoid(x)


def _lane_pick(x, lane):
    idx = lax.broadcasted_iota(jnp.int32, x.shape, 1)
    return jnp.sum(jnp.where(idx == lane, x, 0.0), axis=1, keepdims=True)


def _bias_aug(col, ones_first):
    rows = col.shape[0]
    hi, mid, lo = _split3(col)
    lane = lax.broadcasted_iota(jnp.int32, (rows, LANES), 1)
    p0, p1 = (3, 0) if ones_first else (0, 3)
    out = jnp.where(lane == p0, hi.astype(F32), 0.0)
    out = jnp.where(lane == p0 + 1, mid.astype(F32), out)
    out = jnp.where(lane == p0 + 2, lo.astype(F32), out)
    out = jnp.where((lane >= p1) & (lane < p1 + 3), 1.0, out)
    return _mx(out)


def _shift_rows(x, sh_ref, first_step):
    rows = x.shape[0]

    @pl.when(first_step)
    def _():
        sh_ref[...] = jnp.zeros_like(sh_ref)

    sh_ref[0:8, :] = sh_ref[rows:rows + 8, :]
    sh_ref[8:rows + 8, :] = x
    return sh_ref


def _layer_norm(y, g, b):
    mu = jnp.mean(y, axis=-1, keepdims=True)
    d = y - mu
    var = jnp.mean(d * d, axis=-1, keepdims=True)
    return d * lax.rsqrt(var + LN_EPS) * g + b


def _inproj_kernel(x_ref, w_ref, o_ref, xb_ref):
    @pl.when(pl.program_id(1) == 0)
    def _():
        xb_ref[...] = _mx(x_ref[...])

    o_ref[...] = _dot(xb_ref[...], w_ref[...])


def _inproj(x2d, w_pack):
    t = x2d.shape[0]
    return pl.pallas_call(
        _inproj_kernel,
        grid=(t // ROW_TILE, N_PACK // PROJ_COL_TILE),
        in_specs=[pl.BlockSpec((ROW_TILE, D_MODEL), lambda i, j: (i, 0)),
                  pl.BlockSpec((D_MODEL, PROJ_COL_TILE), lambda i, j: (0, j))],
        out_specs=pl.BlockSpec((ROW_TILE, PROJ_COL_TILE), lambda i, j: (i, j)),
        out_shape=jax.ShapeDtypeStruct((t, N_PACK), F32),
        scratch_shapes=[pltpu.VMEM((ROW_TILE, D_MODEL), MXU_DTYPE)],
        compiler_params=_params(2),
        name="inproj",
    )(x2d, w_pack)


def _fox_prep_kernel(q_ref, k_ref, v_ref, g_ref, bf_ref, tri_ref, qa_ref, ka_ref, vb_ref, carry_ref):
    @pl.when(pl.program_id(1) == 0)
    def _():
        carry_ref[...] = jnp.zeros_like(carry_ref)

    lf = _log_sigmoid(g_ref[...] + bf_ref[...])
    cs = _select_matmul(tri_ref[...], lf) + carry_ref[0:1, :]
    carry_ref[0:1, :] = cs[PREP_TILE - 1:PREP_TILE, :]
    q = q_ref[...] * (FX_DH ** -0.5)
    k = k_ref[...]
    v = v_ref[...]
    for h in range(FX_HEADS):
        sl = slice(h * FX_DH, (h + 1) * FX_DH)
        cf = _lane_pick(cs, h)
        qa_ref[h] = jnp.concatenate([_mx(q[:, sl]), _bias_aug(cf, False)], axis=1)
        ka_ref[h] = jnp.concatenate([_mx(k[:, sl]), _bias_aug(-cf, True)], axis=1)
        vb_ref[h] = _mx(v[:, sl])


def _mlstm_prep_kernel(qk_ref, v_ref, g_ref, gb_ref, conv_ref, tri_ref, qa_ref, ka_ref, vb_ref,
                       carry_ref, sh_ref):
    first = pl.program_id(1) == 0

    @pl.when(first)
    def _():
        carry_ref[...] = jnp.zeros_like(carry_ref)

    gs = g_ref[...] + gb_ref[...]
    lf = _log_sigmoid(gs)
    cs = _select_matmul(tri_ref[...], lf) + carry_ref[0:1, :]
    carry_ref[0:1, :] = cs[PREP_TILE - 1:PREP_TILE, :]

    x = qk_ref[...]
    sh = _shift_rows(x, sh_ref, first)
    acc = conv_ref[0:1, :] * x
    for j in range(1, ML_CONV):
        acc = acc + conv_ref[j:j + 1, :] * sh[8 - j:8 - j + PREP_TILE, :]
    qk = _silu(acc)
    v = v_ref[...]
    for h in range(ML_HEADS):
        sl = slice(h * ML_DH, (h + 1) * ML_DH)
        slk = slice(GW + h * ML_DH, GW + (h + 1) * ML_DH)
        fcol = _lane_pick(cs, ML_HEADS + h)
        icol = _lane_pick(gs, h)
        qa_ref[h] = jnp.concatenate([_mx(qk[:, sl]), _bias_aug(fcol, False)], axis=1)
        ka_ref[h] = jnp.concatenate([_mx(qk[:, slk] * (ML_DH ** -0.5)), _bias_aug(icol - fcol, True)], axis=1)
        vb_ref[h] = _mx(v[:, sl])


def _flash_prep(mode, p3, gate_bias, conv_w, tri):
    b, s, _ = p3.shape
    nh = FX_HEADS
    grid = (b, s // PREP_TILE)
    out_shape = [jax.ShapeDtypeStruct((b, nh, s, 2 * LANES), MXU_DTYPE),
                 jax.ShapeDtypeStruct((b, nh, s, 2 * LANES), MXU_DTYPE),
                 jax.ShapeDtypeStruct((b, nh, s, LANES), MXU_DTYPE)]
    out_specs = [pl.BlockSpec((None, nh, PREP_TILE, 2 * LANES), lambda i, j: (i, 0, j, 0)),
                 pl.BlockSpec((None, nh, PREP_TILE, 2 * LANES), lambda i, j: (i, 0, j, 0)),
                 pl.BlockSpec((None, nh, PREP_TILE, LANES), lambda i, j: (i, 0, j, 0))]

    def col(width, offset):
        return pl.BlockSpec((None, PREP_TILE, width), lambda i, j: (i, j, offset // width))

    const2 = lambda i, j: (0, 0)
    if mode == "fox":
        return pl.pallas_call(
            _fox_prep_kernel, grid=grid,
            in_specs=[col(GW, C_FX_Q), col(GW, C_FX_K), col(GW, C_FX_V), col(LANES, C_FX_G),
                      pl.BlockSpec((1, LANES), const2), pl.BlockSpec((PREP_TILE, PREP_TILE), const2)],
            out_specs=out_specs, out_shape=out_shape,
            scratch_shapes=[pltpu.VMEM((8, LANES), F32)],
            compiler_params=_params(2), name="fox_prep",
        )(p3, p3, p3, p3, gate_bias, tri)
    return pl.pallas_call(
        _mlstm_prep_kernel, grid=grid,
        in_specs=[col(2 * GW, C_ML_Q), col(GW, C_ML_V), col(LANES, C_ML_G),
                  pl.BlockSpec((1, LANES), const2), pl.BlockSpec((8, 2 * GW), const2),
                  pl.BlockSpec((PREP_TILE, PREP_TILE), const2)],
        out_specs=out_specs, out_shape=out_shape,
        scratch_shapes=[pltpu.VMEM((8, LANES), F32), pltpu.VMEM((PREP_TILE + 8, 2 * GW), F32)],
        compiler_params=_params(2), name="mlstm_prep",
    )(p3, p3, p3, gate_bias, conv_w, tri)


def _fox_flash_kernel(qa_ref, ka_ref, v_ref, out_ref):
    i = pl.program_id(2)
    tq = FLASH_TILE
    qa = qa_ref[...]

    def update(carry, s, v):
        m, l, acc = carry
        m_new = jnp.maximum(m, jnp.max(s, axis=1, keepdims=True))
        alpha = jnp.exp(m - m_new)
        p = jnp.exp(s - m_new)
        l = alpha * l + jnp.sum(p, axis=1, keepdims=True)
        acc = alpha * acc + _dot(_mx(p), v)
        return m_new, l, acc

    def body(j, carry):
        off = pl.multiple_of(j * tq, tq)
        s = _dot_nt(qa, ka_ref[pl.ds(off, tq), :])
        return update(carry, s, v_ref[pl.ds(off, tq), :])

    init = (jnp.full((tq, 1), NEG_BIG, F32), jnp.zeros((tq, 1), F32), jnp.zeros((tq, FX_DH), F32))
    carry = lax.fori_loop(0, i, body, init)
    off = pl.multiple_of(i * tq, tq)
    s = _dot_nt(qa, ka_ref[pl.ds(off, tq), :])
    row = lax.broadcasted_iota(jnp.int32, (tq, tq), 0)
    colv = lax.broadcasted_iota(jnp.int32, (tq, tq), 1)
    s = jnp.where(colv <= row, s, NEG_BIG)
    _, l, acc = update(carry, s, v_ref[pl.ds(off, tq), :])
    out_ref[...] = _mx(acc / l)


def _mlstm_flash_kernel(qa_ref, ka_ref, v_ref, o_ref, ng_ref, out_ref):
    i = pl.program_id(2)
    tq = FLASH_TILE
    qa = qa_ref[...]
    q = qa[:, :ML_DH]
    qb = qa[:, ML_DH:]

    def update(carry, qk, d, v):
        m, l, acc = carry
        m_new = jnp.maximum(m, jnp.max(d, axis=1, keepdims=True))
        alpha = jnp.exp(m - m_new)
        p = jnp.exp(d - m_new) * qk
        l = alpha * l + jnp.sum(p, axis=1, keepdims=True)
        acc = alpha * acc + _dot(_mx(p), v)
        return m_new, l, acc

    def tiles(off):
        ka = ka_ref[pl.ds(off, tq), :]
        return _dot_nt(q, ka[:, :ML_DH]), _dot_nt(qb, ka[:, ML_DH:]), v_ref[pl.ds(off, tq), :]

    def body(j, carry):
        qk, d, v = tiles(pl.multiple_of(j * tq, tq))
        return update(carry, qk, d, v)

    lane = lax.broadcasted_iota(jnp.int32, (tq, LANES), 1)
    f_row = jnp.sum(jnp.where(lane < 3, qb.astype(F32), 0.0), axis=1, keepdims=True)
    init = (f_row, jnp.zeros((tq, 1), F32), jnp.zeros((tq, ML_DH), F32))
    carry = lax.fori_loop(0, i, body, init)
    qk, d, v = tiles(pl.multiple_of(i * tq, tq))
    row = lax.broadcasted_iota(jnp.int32, (tq, tq), 0)
    colv = lax.broadcasted_iota(jnp.int32, (tq, tq), 1)
    d = jnp.where(colv <= row, d, NEG_BIG)
    m, l, acc = update(carry, qk, d, v)
    h = acc / jnp.maximum(jnp.abs(l), jnp.exp(-m))
    h = h * lax.rsqrt(jnp.mean(h * h, axis=1, keepdims=True) + NORM_EPS) * ng_ref[...]
    out_ref[...] = _mx(_sigmoid(o_ref[...]) * h)


def _flash(mode, qa, ka, vb, p3, norm_g):
    b, nh, s, _ = qa.shape
    tq = FLASH_TILE
    grid = (b, nh, s // tq)
    qspec = pl.BlockSpec((None, None, tq, 2 * LANES), lambda bi, h, i: (bi, h, i, 0))
    kspec = pl.BlockSpec((None, None, s, 2 * LANES), lambda bi, h, i: (bi, h, 0, 0))
    vspec = pl.BlockSpec((None, None, s, LANES), lambda bi, h, i: (bi, h, 0, 0))
    ospec = pl.BlockSpec((None, tq, LANES), lambda bi, h, i: (bi, i, h))
    oshape = jax.ShapeDtypeStruct((b, s, GW), MXU_DTYPE)
    if mode == "fox":
        return pl.pallas_call(
            _fox_flash_kernel, grid=grid, in_specs=[qspec, kspec, vspec], out_specs=ospec, out_shape=oshape,
            compiler_params=_params(3), name="fox_flash",
        )(qa, ka, vb)
    gate = pl.BlockSpec((None, tq, LANES), lambda bi, h, i: (bi, i, C_ML_O // LANES + h))
    ng = pl.BlockSpec((1, LANES), lambda bi, h, i: (0, h))
    return pl.pallas_call(
        _mlstm_flash_kernel, grid=grid, in_specs=[qspec, kspec, vspec, gate, ng],
        out_specs=ospec, out_shape=oshape, compiler_params=_params(3), name="mlstm_flash",
    )(qa, ka, vb, p3, norm_g)


def _hgrn2_kernel(q_ref, z_ref, i_ref, g_ref, lb_ref, ng_ref, mats_ref, out_ref, state_ref):
    L = HG_CHUNK

    @pl.when(pl.program_id(2) == 0)
    def _():
        state_ref[...] = jnp.zeros_like(state_ref)

    lb = lb_ref[...]
    z = z_ref[...]
    q = _silu(q_ref[...])
    v = _silu(i_ref[...])
    lf = _log_sigmoid(z) + jnp.log1p(lb * jnp.exp(jnp.minimum(-z, EXP_CLIP)))
    k = (1.0 - lb) * _sigmoid(-z)
    sums = _select_matmul(mats_ref[...], lf)
    b = sums[0:L]
    sfx = sums[L:2 * L]
    row = lax.broadcasted_iota(jnp.int32, (L, L), 0)
    colv = lax.broadcasted_iota(jnp.int32, (L, L), 1)
    diff = row ^ colv
    att = jnp.zeros((L, L), F32)
    for lev in range(HG_LEVELS):
        dq = sums[(2 + lev) * L:(3 + lev) * L]
        dk = sums[(2 + HG_LEVELS + lev) * L:(3 + HG_LEVELS + lev) * L]
        p = _dot_nt(_mx(q * jnp.exp(dq)), _mx(k * jnp.exp(dk)))
        att = jnp.where(((diff >> lev) == 1) & (row > colv), p, att)
    vb = _mx(v)
    st = state_ref[...]
    o = _dot(_mx(att), vb) + jnp.sum(q * k, axis=1, keepdims=True) * v
    o = o + _dot_nt(_mx(q * jnp.exp(b)), _mx(st))
    b_last = b[L - 1:L, :]
    state_ref[...] = st * jnp.exp(b_last) + _dot_tn(vb, _mx(k * jnp.exp(sfx)))
    o = o * _sigmoid(g_ref[...])
    out_ref[...] = _mx(o * lax.rsqrt(jnp.mean(o * o, axis=1, keepdims=True) + NORM_EPS) * ng_ref[...])


def _hgrn2_mats():
    L = HG_CHUNK
    t = np.arange(L)[:, None]
    j = np.arange(L)[None, :]
    blocks = [j <= t, j > t]
    for lev in range(HG_LEVELS):
        start = (t >> lev) << lev
        blocks.append((j > start) & (j <= t))
    for lev in range(HG_LEVELS):
        nxt = ((t >> lev) + 1) << lev
        blocks.append((j > t) & (j <= nxt))
    return jnp.asarray(np.concatenate(blocks, axis=0).astype(np.float32), dtype=MXU_DTYPE)


def _hgrn2(p3, lb, norm_g):
    b, s, _ = p3.shape
    L = HG_CHUNK

    def col(offset):
        return pl.BlockSpec((None, L, LANES), lambda bi, h, c: (bi, c, offset // LANES + h))

    hvec = pl.BlockSpec((1, LANES), lambda bi, h, c: (0, h))
    mats = _hgrn2_mats()
    return pl.pallas_call(
        _hgrn2_kernel, grid=(b, HG_HEADS, s // L),
        in_specs=[col(C_HG_Q), col(C_HG_Z), col(C_HG_I), col(C_HG_G), hvec, hvec,
                  pl.BlockSpec(mats.shape, lambda bi, h, c: (0, 0))],
        out_specs=pl.BlockSpec((None, L, LANES), lambda bi, h, c: (bi, c, h)),
        out_shape=jax.ShapeDtypeStruct((b, s, GW), MXU_DTYPE),
        scratch_shapes=[pltpu.VMEM((LANES, LANES), F32)],
        compiler_params=_params(3), name="hgrn2",
    )(p3, p3, p3, p3, lb, norm_g, mats)


def _rwkv_kernel(rkv_ref, gd_ref, wa_ref, mu_rkv, mu_gd, mu_wa, w0_ref, a0_ref, kk_ref, ka_ref, rk_ref,
                 gng_ref, gnb_ref, w2_ref, a2_ref, g2_ref, tri_ref, out_ref,
                 sh_rkv, sh_gd, sh_wa, state_ref):
    L = RW_CHUNK
    first = pl.program_id(1) == 0

    @pl.when(first)
    def _():
        state_ref[...] = jnp.zeros_like(state_ref)

    def lerp(x_ref, sh_ref, mu_ref):
        x = x_ref[...]
        sh = _shift_rows(x, sh_ref, first)
        return x + (sh[7:7 + L, :] - x) * mu_ref[...]

    rkv = lerp(rkv_ref, sh_rkv, mu_rkv)
    gd = lerp(gd_ref, sh_gd, mu_gd)
    wa = lerp(wa_ref, sh_wa, mu_wa)
    r_all, k_all, v_all = rkv[:, :RW_W], rkv[:, RW_W:2 * RW_W], rkv[:, 2 * RW_W:]
    wd, ad = wa[:, :LANES], wa[:, LANES:]

    wpre = w0_ref[...] + _dot(_mx(jnp.tanh(wd)), w2_ref[...])
    wlog_all = -jnp.exp(-_softplus(-wpre) - 0.5)
    a_all = _sigmoid(a0_ref[...] + _dot(_mx(ad), a2_ref[...]))
    g_all = _dot(_mx(_sigmoid(gd)), g2_ref[...])

    tri = tri_ref[...]
    row = lax.broadcasted_iota(jnp.int32, (L, L), 0)
    colv = lax.broadcasted_iota(jnp.int32, (L, L), 1)
    diff = row ^ colv
    strict = row > colv
    causal = row >= colv
    eye = row == colv
    lane = lax.broadcasted_iota(jnp.int32, (L, RW_PAD), 1)
    valid = lane < RW_DH
    mid = L // 2

    outs = []
    for h in range(RW_HEADS):
        sl = slice(h * RW_PAD, (h + 1) * RW_PAD)
        r, k, v, a, wlog = r_all[:, sl], k_all[:, sl], v_all[:, sl], a_all[:, sl], wlog_all[:, sl]
        kk = k * kk_ref[:, sl]
        kk = kk * lax.rsqrt(jnp.maximum(jnp.sum(kk * kk, axis=1, keepdims=True), 1e-24))
        beta = kk * a
        kt = k * (1.0 + (a - 1.0) * ka_ref[:, sl])

        b = _select_matmul(tri, wlog)
        bp = b - wlog
        bm = b[mid:mid + 1, :]
        bl = b[L - 1:L, :]
        e_in = jnp.exp(b - bm)
        e_out = jnp.exp(bm - b)
        lhs = _mx(jnp.concatenate([kk * jnp.exp(bp - bm), r * e_in], axis=0))
        pb = _dot_nt(lhs, _mx(beta * e_out))
        pk = _dot_nt(lhs, _mx(kt * e_out))
        amat = jnp.where(strict, pb[:L], 0.0)
        dmat = jnp.where(causal, pb[L:], 0.0)
        bmat = jnp.where(strict, pk[:L], 0.0)
        cmat = jnp.where(causal, pk[L:], 0.0)

        tinv = jnp.where(eye, 1.0, 0.0) - jnp.where(diff == 1, amat, 0.0)
        for lev in range(1, RW_LEVELS):
            a_lev = jnp.where((diff >> lev) == 1, amat, 0.0)
            tinv = tinv - _dot_hi(_dot_hi(tinv, a_lev), tinv)

        vb = _mx(v)
        kd = kk * jnp.exp(bp)
        uk = _dot_hi(tinv, jnp.concatenate([_dot(_mx(bmat), vb), kd], axis=1))
        yr = jnp.concatenate([_dot(_mx(cmat), vb), r * jnp.exp(b)], axis=1) - _dot(_mx(dmat), _mx(uk))
        y_hat, r_hat = yr[:, :RW_PAD], yr[:, RW_PAD:]
        u_hat, k_hat = uk[:, :RW_PAD], uk[:, RW_PAD:]

        e_end = jnp.exp(bl - b)
        kte = kt * e_end
        be = beta * e_end
        st = state_ref[h]
        y = _dot_nt(_mx(r_hat), _mx(st)) + y_hat
        gmat = jnp.where(eye, jnp.exp(bl), 0.0) - _dot_tn(_mx(k_hat), _mx(be))
        hmat = _dot_tn(vb, _mx(kte)) - _dot_tn(_mx(u_hat), _mx(be))
        state_ref[h] = _dot_hi(st, gmat) + hmat

        mu_y = jnp.sum(y, axis=1, keepdims=True) * (1.0 / RW_DH)
        dy = jnp.where(valid, y - mu_y, 0.0)
        var_y = jnp.sum(dy * dy, axis=1, keepdims=True) * (1.0 / RW_DH)
        yn = dy * lax.rsqrt(var_y + RW_GN_EPS) * gng_ref[:, sl] + gnb_ref[:, sl]
        bonus = jnp.sum(r * kt * rk_ref[:, sl], axis=1, keepdims=True) * v
        outs.append((yn + bonus) * g_all[:, sl])
    out_ref[...] = _mx(jnp.concatenate(outs, axis=1))


def _rwkv(p3, rp, tri):
    b, s, _ = p3.shape
    L = RW_CHUNK
    c2 = lambda bi, c: (0, 0)

    def vec(width):
        return pl.BlockSpec((1, width), c2)

    return pl.pallas_call(
        _rwkv_kernel, grid=(b, s // L),
        in_specs=[pl.BlockSpec((None, L, 3 * RW_W), lambda bi, c: (bi, c, 0)),
                  pl.BlockSpec((None, L, 2 * LANES), lambda bi, c: (bi, c, C_RW_GD // (2 * LANES))),
                  pl.BlockSpec((None, L, 2 * LANES), lambda bi, c: (bi, c, C_RW_WD // (2 * LANES))),
                  vec(3 * RW_W), vec(2 * LANES), vec(2 * LANES),
                  vec(RW_W), vec(RW_W), vec(RW_W), vec(RW_W), vec(RW_W), vec(RW_W), vec(RW_W),
                  pl.BlockSpec((LANES, RW_W), c2), pl.BlockSpec((LANES, RW_W), c2),
                  pl.BlockSpec((2 * LANES, RW_W), c2), pl.BlockSpec((L, L), c2)],
        out_specs=pl.BlockSpec((None, L, RW_W), lambda bi, c: (bi, c, 0)),
        out_shape=jax.ShapeDtypeStruct((b, s, RW_W), MXU_DTYPE),
        scratch_shapes=[pltpu.VMEM((L + 8, 3 * RW_W), F32), pltpu.VMEM((L + 8, 2 * LANES), F32),
                        pltpu.VMEM((L + 8, 2 * LANES), F32), pltpu.VMEM((RW_HEADS, RW_PAD, RW_PAD), F32)],
        compiler_params=_params(2), name="rwkv7",
    )(p3, p3, p3, rp["mu_rkv"], rp["mu_gd"], rp["mu_wa"], rp["w0"], rp["a0"], rp["k_k"], rp["k_a"], rp["r_k"],
      rp["gn_g"], rp["gn_b"], rp["w2"], rp["a2"], rp["g2"], tri)


def _outproj_kernel(yml_ref, yrw_ref, yfx_ref, yhg_ref, wml_ref, wrw_ref, wfx_ref, whg_ref,
                    x_ref, g_ref, b_ref, o_ref):
    mix = _dot(yml_ref[...], wml_ref[...]) + _dot(yrw_ref[...], wrw_ref[...])
    mix = mix + _dot(yfx_ref[...], wfx_ref[...]) + _dot(yhg_ref[...], whg_ref[...])
    o_ref[...] = _layer_norm(ALPHA * x_ref[...] + mix, g_ref[...], b_ref[...])


def _outproj(y_ml, y_rw, y_fx, y_hg, wo, x2d, g, bvec):
    t = x2d.shape[0]
    tm = ROW_TILE // 2
    row = lambda w: pl.BlockSpec((tm, w), lambda i: (i, 0))
    full = lambda a: pl.BlockSpec(a.shape, lambda i: (0, 0))
    return pl.pallas_call(
        _outproj_kernel, grid=(t // tm,),
        in_specs=[row(GW), row(RW_W), row(GW), row(GW), full(wo["ml"]), full(wo["rw"]), full(wo["fx"]),
                  full(wo["hg"]), row(D_MODEL), full(g), full(bvec)],
        out_specs=row(D_MODEL), out_shape=jax.ShapeDtypeStruct((t, D_MODEL), F32),
        compiler_params=_params(1), name="outproj_ln",
    )(y_ml, y_rw, y_fx, y_hg, wo["ml"], wo["rw"], wo["fx"], wo["hg"], x2d, g, bvec)


def _ffn_kernel(x_ref, wg_ref, wu_ref, wd_ref, g_ref, b_ref, o_ref, xb_ref, acc_ref):
    j = pl.program_id(1)

    @pl.when(j == 0)
    def _():
        xb_ref[...] = _mx(x_ref[...])
        acc_ref[...] = jnp.zeros_like(acc_ref)

    xb = xb_ref[...]
    h = _silu(_dot(xb, wg_ref[...])) * _dot(xb, wu_ref[...])
    acc_ref[...] += _dot(_mx(h), wd_ref[...])

    @pl.when(j == pl.num_programs(1) - 1)
    def _():
        o_ref[...] = _layer_norm(ALPHA * x_ref[...] + acc_ref[...], g_ref[...], b_ref[...])


def _ffn(x2d, wg, wu, wd, g, bvec):
    t = x2d.shape[0]
    tm, tf = ROW_TILE, FF_TILE
    vec = pl.BlockSpec((1, D_MODEL), lambda i, j: (0, 0))
    return pl.pallas_call(
        _ffn_kernel, grid=(t // tm, D_FF_PAD // tf),
        in_specs=[pl.BlockSpec((tm, D_MODEL), lambda i, j: (i, 0)),
                  pl.BlockSpec((D_MODEL, tf), lambda i, j: (0, j)),
                  pl.BlockSpec((D_MODEL, tf), lambda i, j: (0, j)),
                  pl.BlockSpec((tf, D_MODEL), lambda i, j: (j, 0)), vec, vec],
        out_specs=pl.BlockSpec((tm, D_MODEL), lambda i, j: (i, 0)),
        out_shape=jax.ShapeDtypeStruct((t, D_MODEL), F32),
        scratch_shapes=[pltpu.VMEM((tm, D_MODEL), MXU_DTYPE), pltpu.VMEM((tm, D_MODEL), F32)],
        compiler_params=_params(2), name="ffn_ln",
    )(x2d, wg, wu, wd, g, bvec)


def _router_kernel(x_ref, w_ref, comb_ref):
    logits = _dot_hi(x_ref[...], w_ref[...])
    lane = lax.broadcasted_iota(jnp.int32, logits.shape, 1)
    valid = lane < N_EXPERTS
    logits = jnp.where(valid, logits, NEG_BIG)
    e = jnp.exp(logits - jnp.max(logits, axis=1, keepdims=True))
    probs = jnp.where(valid, e / jnp.sum(e, axis=1, keepdims=True), -1.0)
    p1 = jnp.max(probs, axis=1, keepdims=True)
    i1 = jnp.min(jnp.where(probs == p1, lane, LANES), axis=1, keepdims=True)
    rest = jnp.where(lane == i1, -1.0, probs)
    p2 = jnp.max(rest, axis=1, keepdims=True)
    i2 = jnp.min(jnp.where(rest == p2, lane, LANES), axis=1, keepdims=True)
    tot = p1 + p2
    comb_ref[...] = jnp.where(lane == i1, p1 / tot, 0.0) + jnp.where(lane == i2, p2 / tot, 0.0)


def _router(x2d, w_router):
    t = x2d.shape[0]
    tm = ROW_TILE
    return pl.pallas_call(
        _router_kernel, grid=(t // tm,),
        in_specs=[pl.BlockSpec((tm, D_MODEL), lambda i: (i, 0)), pl.BlockSpec((D_MODEL, LANES), lambda i: (0, 0))],
        out_specs=pl.BlockSpec((tm, LANES), lambda i: (i, 0)),
        out_shape=jax.ShapeDtypeStruct((t, LANES), F32),
        compiler_params=_params(1), name="moe_router",
    )(x2d, w_router)


def _moe_kernel(x_ref, comb_ref, wg_ref, wu_ref, wd_ref, g_ref, b_ref, o_ref, xb_ref, acc_ref):
    e = pl.program_id(1)
    j = pl.program_id(2)

    @pl.when((e == 0) & (j == 0))
    def _():
        xb_ref[...] = _mx(x_ref[...])
        acc_ref[...] = jnp.zeros_like(acc_ref)

    xb = xb_ref[...]
    comb = comb_ref[...]
    lane = lax.broadcasted_iota(jnp.int32, comb.shape, 1)
    ce = jnp.sum(jnp.where(lane == e, comb, 0.0), axis=1, keepdims=True)
    h = _silu(_dot(xb, wg_ref[...])) * _dot(xb, wu_ref[...])
    acc_ref[...] += ce * _dot(_mx(h), wd_ref[...])

    @pl.when((e == pl.num_programs(1) - 1) & (j == pl.num_programs(2) - 1))
    def _():
        o_ref[...] = _layer_norm(ALPHA * x_ref[...] + acc_ref[...], g_ref[...], b_ref[...])


def _moe(x2d, comb, wg, wu, wd, g, bvec):
    t = x2d.shape[0]
    tm, tf = ROW_TILE, FF_TILE
    vec = pl.BlockSpec((1, D_MODEL), lambda i, e, j: (0, 0))
    return pl.pallas_call(
        _moe_kernel, grid=(t // tm, N_EXPERTS, D_FF_PAD // tf),
        in_specs=[pl.BlockSpec((tm, D_MODEL), lambda i, e, j: (i, 0)),
                  pl.BlockSpec((tm, LANES), lambda i, e, j: (i, 0)),
                  pl.BlockSpec((None, D_MODEL, tf), lambda i, e, j: (e, 0, j)),
                  pl.BlockSpec((None, D_MODEL, tf), lambda i, e, j: (e, 0, j)),
                  pl.BlockSpec((None, tf, D_MODEL), lambda i, e, j: (e, j, 0)), vec, vec],
        out_specs=pl.BlockSpec((tm, D_MODEL), lambda i, e, j: (i, 0)),
        out_shape=jax.ShapeDtypeStruct((t, D_MODEL), F32),
        scratch_shapes=[pltpu.VMEM((tm, D_MODEL), MXU_DTYPE), pltpu.VMEM((tm, D_MODEL), F32)],
        compiler_params=_params(3), name="moe_ln",
    )(x2d, comb, wg, wu, wd, g, bvec)


def _pad_cols(a, width):
    return jnp.pad(a, ((0, 0), (0, width - a.shape[1])))


def _pad_heads(a):
    lead = a.shape[:-1]
    a = a.reshape(lead + (RW_HEADS, RW_DH))
    a = jnp.pad(a, [(0, 0)] * len(lead) + [(0, 0), (0, RW_PAD - RW_DH)])
    return a.reshape(lead + (RW_W,))


def _pack_w_in(w):
    ml_n = 4 * GW + 2 * ML_HEADS
    rw_n = 3 * GW + RW_DECAY_LORA + RW_A_LORA + RW_GATE_LORA
    fx_n = 3 * GW + FX_HEADS
    ml, rw, fx, hg = w[:, :ml_n], w[:, ml_n:ml_n + rw_n], w[:, ml_n + rw_n:ml_n + rw_n + fx_n], w[:, ml_n + rw_n + fx_n:]
    o = 3 * GW
    pieces = [_pad_heads(rw[:, :GW]), _pad_heads(rw[:, GW:2 * GW]), _pad_heads(rw[:, 2 * GW:3 * GW]),
              ml[:, :4 * GW], fx[:, :3 * GW], hg,
              rw[:, o + RW_DECAY_LORA + RW_A_LORA:],
              _pad_cols(ml[:, 4 * GW:], LANES), _pad_cols(fx[:, 3 * GW:], LANES),
              _pad_cols(rw[:, o:o + RW_DECAY_LORA], LANES),
              _pad_cols(rw[:, o + RW_DECAY_LORA:o + RW_DECAY_LORA + RW_A_LORA], LANES)]
    return jnp.concatenate(pieces, axis=1).astype(MXU_DTYPE)


def _row(a, width=None):
    a = a.reshape(1, -1).astype(F32)
    return a if width is None else _pad_cols(a, width)


def _pad_rows(a, rows):
    return jnp.pad(a, ((0, rows - a.shape[0]), (0, 0)))


def _rwkv_params(mu, w0, w2, a0, a2, g2, k_k, k_a, r_k, gn_g, gn_b):
    o = 3 * GW
    mu_rkv = jnp.concatenate([_pad_heads(mu[:GW]), _pad_heads(mu[GW:2 * GW]), _pad_heads(mu[2 * GW:o])])
    mu_wd = jnp.pad(mu[o:o + RW_DECAY_LORA], (0, LANES - RW_DECAY_LORA))
    mu_ad = jnp.pad(mu[o + RW_DECAY_LORA:o + RW_DECAY_LORA + RW_A_LORA], (0, LANES - RW_A_LORA))
    return {
        "mu_rkv": _row(mu_rkv), "mu_gd": _row(mu[o + RW_DECAY_LORA + RW_A_LORA:]),
        "mu_wa": _row(jnp.concatenate([mu_wd, mu_ad])),
        "w0": _row(_pad_heads(w0)), "a0": _row(_pad_heads(a0)), "k_k": _row(_pad_heads(k_k)),
        "k_a": _row(_pad_heads(k_a)), "r_k": _row(_pad_heads(r_k.reshape(-1))),
        "gn_g": _row(_pad_heads(gn_g)), "gn_b": _row(_pad_heads(gn_b)),
        "w2": _pad_rows(_pad_heads(w2), LANES).astype(MXU_DTYPE),
        "a2": _pad_rows(_pad_heads(a2), LANES).astype(MXU_DTYPE),
        "g2": _pad_heads(g2).astype(MXU_DTYPE),
    }


def _pack_w_out(w):
    rw = w[GW:2 * GW].reshape(RW_HEADS, RW_DH, D_MODEL)
    rw = jnp.pad(rw, ((0, 0), (0, RW_PAD - RW_DH), (0, 0))).reshape(RW_W, D_MODEL)
    return {"ml": w[:GW].astype(MXU_DTYPE), "rw": rw.astype(MXU_DTYPE),
            "fx": w[2 * GW:3 * GW].astype(MXU_DTYPE), "hg": w[3 * GW:].astype(MXU_DTYPE)}


def _pad_ff(wg, wu, wd):
    padc = [(0, 0)] * (wg.ndim - 1) + [(0, D_FF_PAD - D_FF)]
    padr = [(0, 0)] * (wd.ndim - 2) + [(0, D_FF_PAD - D_FF), (0, 0)]
    return (jnp.pad(wg, padc).astype(MXU_DTYPE), jnp.pad(wu, padc).astype(MXU_DTYPE),
            jnp.pad(wd, padr).astype(MXU_DTYPE))


def _tri(n):
    return jnp.asarray(np.tril(np.ones((n, n), np.float32)), dtype=MXU_DTYPE)


def kernel(x, w_in, w_out, ln1_g, ln1_b, ln2_g, ln2_b, ml_conv, ml_b_i, ml_b_f, ml_norm_g, rw_mu, rw_w0, rw_w2, rw_a0, rw_a2, rw_g2, rw_k_k, rw_k_a, rw_r_k, rw_gn_g, rw_gn_b, fx_b_f, hg_lb_logits, hg_norm_g, ffn_w_gate, ffn_w_up, ffn_w_down, moe_router, moe_w_gate, moe_w_up, moe_w_down):
    bsz, seq, d = x.shape
    depth = w_in.shape[0]
    assert d == D_MODEL and seq % FLASH_TILE == 0 and (bsz * seq) % ROW_TILE == 0
    lb_probs = jax.nn.softmax(hg_lb_logits.astype(F32), axis=0)
    lower_bounds = jnp.cumsum(lb_probs, axis=0) - lb_probs[0]
    tri_prep = _tri(PREP_TILE)
    tri_rw = _tri(RW_CHUNK)
    x2d = x.reshape(bsz * seq, d)
    for l in range(depth):
        p3 = _inproj(x2d, _pack_w_in(w_in[l])).reshape(bsz, seq, N_PACK)

        ml_gate_bias = _row(jnp.concatenate([ml_b_i[l], ml_b_f[l]]), LANES)
        qa, ka, vb = _flash_prep("mlstm", p3, ml_gate_bias, _pad_rows(ml_conv[l], 8), tri_prep)
        y_ml = _flash("mlstm", qa, ka, vb, p3, _row(ml_norm_g[l]))

        qa, ka, vb = _flash_prep("fox", p3, _row(fx_b_f[l], LANES), None, tri_prep)
        y_fx = _flash("fox", qa, ka, vb, None, None)

        y_hg = _hgrn2(p3, _row(lower_bounds[l]), _row(hg_norm_g[l]))

        rp = _rwkv_params(rw_mu[l], rw_w0[l], rw_w2[l], rw_a0[l], rw_a2[l], rw_g2[l], rw_k_k[l], rw_k_a[l],
                          rw_r_k[l], rw_gn_g[l], rw_gn_b[l])
        y_rw = _rwkv(p3, rp, tri_rw)

        t = bsz * seq
        x2d = _outproj(y_ml.reshape(t, GW), y_rw.reshape(t, RW_W), y_fx.reshape(t, GW), y_hg.reshape(t, GW),
                       _pack_w_out(w_out[l]), x2d, _row(ln1_g[l]), _row(ln1_b[l]))
        j = l // 2
        if l % 2 == 0:
            wg, wu, wd = _pad_ff(ffn_w_gate[j], ffn_w_up[j], ffn_w_down[j])
            x2d = _ffn(x2d, wg, wu, wd, _row(ln2_g[l]), _row(ln2_b[l]))
        else:
            comb = _router(x2d, _pad_cols(moe_router[j].astype(F32), LANES))
            wg, wu, wd = _pad_ff(moe_w_gate[j], moe_w_up[j], moe_w_down[j])
            x2d = _moe(x2d, comb, wg, wu, wd, _row(ln2_g[l]), _row(ln2_b[l]))
    return x2d.reshape(bsz, seq, d)
```

```python
import functools

import jax
import jax.numpy as jnp
import numpy as np
from jax import lax
from jax.experimental import pallas as pl
from jax.experimental.pallas import tpu as pltpu

F32 = jnp.float32
MXU_DTYPE = jnp.bfloat16
HIGHEST = lax.Precision.HIGHEST

D_MODEL = 2048
N_GROUPS = 4
GW = D_MODEL // N_GROUPS
ML_HEADS = 4
ML_DH = GW // ML_HEADS
ML_CONV = 4
RW_DH = 64
RW_HEADS = GW // RW_DH
RW_DECAY_LORA = 96
RW_A_LORA = 96
RW_GATE_LORA = 256
FX_HEADS = 4
FX_DH = GW // FX_HEADS
HG_HEADS = 4
D_FF = 5504
N_EXPERTS = 8
DEPTH = 2
ALPHA = (2 * DEPTH) ** 0.25
LN_EPS = 1e-5
NORM_EPS = 1e-6
RW_GN_EPS = 64e-5
NEG_BIG = -1e30
EXP_CLIP = 60.0

LANES = 128
RW_PAD = LANES
RW_W = RW_HEADS * RW_PAD
D_FF_PAD = 5632
FF_TILE = 512
ROW_TILE = 512
PROJ_COL_TILE = 256
FLASH_TILE = 512
PREP_TILE = 512
HG_CHUNK = 128
HG_LEVELS = 7
RW_CHUNK = 128
RW_LEVELS = 7
GATHER_ROWS = 512
COMBINE_ROWS = 256
VMEM_LIMIT = 56 * 1024 * 1024

C_RW_R, C_RW_K, C_RW_V = 0, 1024, 2048
C_ML_Q, C_ML_K, C_ML_V, C_ML_O = 3072, 3584, 4096, 4608
C_FX_Q, C_FX_K, C_FX_V = 5120, 5632, 6144
C_HG_Q, C_HG_Z, C_HG_I, C_HG_G = 6656, 7168, 7680, 8192
C_RW_GD = 8704
C_ML_G = 8960
C_FX_G = 9088
C_RW_WD, C_RW_AD = 9216, 9344
N_PACK = 9472


def _params(n_grid):
    return pltpu.CompilerParams(dimension_semantics=("arbitrary",) * n_grid, vmem_limit_bytes=VMEM_LIMIT)


def _dot(a, b):
    return jnp.dot(a, b, preferred_element_type=F32)


def _dot_nt(a, b):
    return lax.dot_general(a, b, (((1,), (1,)), ((), ())), preferred_element_type=F32)


def _dot_tn(a, b):
    return lax.dot_general(a, b, (((0,), (0,)), ((), ())), preferred_element_type=F32)


def _dot_hi(a, b):
    return jnp.dot(a, b, precision=HIGHEST, preferred_element_type=F32)


def _mx(x):
    return x.astype(MXU_DTYPE)


def _split3(x):
    hi = _mx(x)
    r1 = x - hi.astype(F32)
    mid = _mx(r1)
    lo = _mx(r1 - mid.astype(F32))
    return hi, mid, lo


def _select_matmul(mat, x):
    c = x.shape[1]
    hi, mid, lo = _split3(x)
    r = _dot(mat, jnp.concatenate([hi, mid, lo], axis=1))
    return (r[:, :c] + r[:, c:2 * c]) + r[:, 2 * c:]


def _softplus(x):
    return jnp.maximum(x, 0.0) + jnp.log1p(jnp.exp(-jnp.abs(x)))


def _log_sigmoid(x):
    return -_softplus(-x)


def _sigmoid(x):
    return jax.nn.sigmoid(x)


def _silu(x):
    return x * _sigmoid(x)


def _lane_pick(x, lane):
    idx = lax.broadcasted_iota(jnp.int32, x.shape, 1)
    return jnp.sum(jnp.where(idx == lane, x, 0.0), axis=1, keepdims=True)


def _bias_aug(col, ones_first):
    rows = col.shape[0]
    hi, mid, lo = _split3(col)
    lane = lax.broadcasted_iota(jnp.int32, (rows, LANES), 1)
    p0, p1 = (3, 0) if ones_first else (0, 3)
    out = jnp.where(lane == p0, hi.astype(F32), 0.0)
    out = jnp.where(lane == p0 + 1, mid.astype(F32), out)
    out = jnp.where(lane == p0 + 2, lo.astype(F32), out)
    out = jnp.where((lane >= p1) & (lane < p1 + 3), 1.0, out)
    return _mx(out)


def _shift_rows(x, sh_ref, first_step):
    rows = x.shape[0]

    @pl.when(first_step)
    def _():
        sh_ref[...] = jnp.zeros_like(sh_ref)

    sh_ref[0:8, :] = sh_ref[rows:rows + 8, :]
    sh_ref[8:rows + 8, :] = x
    return sh_ref


def _layer_norm(y, g, b):
    mu = jnp.mean(y, axis=-1, keepdims=True)
    d = y - mu
    var = jnp.mean(d * d, axis=-1, keepdims=True)
    return d * lax.rsqrt(var + LN_EPS) * g + b


def _inproj_kernel(x_ref, w_ref, o_ref, xb_ref):
    @pl.when(pl.program_id(1) == 0)
    def _():
        xb_ref[...] = _mx(x_ref[...])

    o_ref[...] = _dot(xb_ref[...], w_ref[...])


def _inproj(x2d, w_pack):
    t = x2d.shape[0]
    return pl.pallas_call(
        _inproj_kernel,
        grid=(t // ROW_TILE, N_PACK // PROJ_COL_TILE),
        in_specs=[pl.BlockSpec((ROW_TILE, D_MODEL), lambda i, j: (i, 0)),
                  pl.BlockSpec((D_MODEL, PROJ_COL_TILE), lambda i, j: (0, j))],
        out_specs=pl.BlockSpec((ROW_TILE, PROJ_COL_TILE), lambda i, j: (i, j)),
        out_shape=jax.ShapeDtypeStruct((t, N_PACK), F32),
        scratch_shapes=[pltpu.VMEM((ROW_TILE, D_MODEL), MXU_DTYPE)],
        compiler_params=_params(2),
        name="inproj",
    )(x2d, w_pack)


def _fox_prep_kernel(q_ref, k_ref, v_ref, g_ref, bf_ref, tri_ref, qa_ref, ka_ref, vb_ref, carry_ref):
    @pl.when(pl.program_id(1) == 0)
    def _():
        carry_ref[...] = jnp.zeros_like(carry_ref)

    lf = _log_sigmoid(g_ref[...] + bf_ref[...])
    cs = _select_matmul(tri_ref[...], lf) + carry_ref[0:1, :]
    carry_ref[0:1, :] = cs[PREP_TILE - 1:PREP_TILE, :]
    q = q_ref[...] * (FX_DH ** -0.5)
    k = k_ref[...]
    v = v_ref[...]
    for h in range(FX_HEADS):
        sl = slice(h * FX_DH, (h + 1) * FX_DH)
        cf = _lane_pick(cs, h)
        qa_ref[h] = jnp.concatenate([_mx(q[:, sl]), _bias_aug(cf, False)], axis=1)
        ka_ref[h] = jnp.concatenate([_mx(k[:, sl]), _bias_aug(-cf, True)], axis=1)
        vb_ref[h] = _mx(v[:, sl])


def _mlstm_prep_kernel(qk_ref, v_ref, g_ref, gb_ref, conv_ref, tri_ref, qa_ref, ka_ref, vb_ref,
                       carry_ref, sh_ref):
    first = pl.program_id(1) == 0

    @pl.when(first)
    def _():
        carry_ref[...] = jnp.zeros_like(carry_ref)

    gs = g_ref[...] + gb_ref[...]
    lf = _log_sigmoid(gs)
    cs = _select_matmul(tri_ref[...], lf) + carry_ref[0:1, :]
    carry_ref[0:1, :] = cs[PREP_TILE - 1:PREP_TILE, :]

    x = qk_ref[...]
    sh = _shift_rows(x, sh_ref, first)
    acc = conv_ref[0:1, :] * x
    for j in range(1, ML_CONV):
        acc = acc + conv_ref[j:j + 1, :] * sh[8 - j:8 - j + PREP_TILE, :]
    qk = _silu(acc)
    v = v_ref[...]
    for h in range(ML_HEADS):
        sl = slice(h * ML_DH, (h + 1) * ML_DH)
        slk = slice(GW + h * ML_DH, GW + (h + 1) * ML_DH)
        fcol = _lane_pick(cs, ML_HEADS + h)
        icol = _lane_pick(gs, h)
        qa_ref[h] = jnp.concatenate([_mx(qk[:, sl]), _bias_aug(fcol, False)], axis=1)
        ka_ref[h] = jnp.concatenate([_mx(qk[:, slk] * (ML_DH ** -0.5)), _bias_aug(icol - fcol, True)], axis=1)
        vb_ref[h] = _mx(v[:, sl])


def _flash_prep(mode, p3, gate_bias, conv_w, tri):
    b, s, _ = p3.shape
    nh = FX_HEADS
    grid = (b, s // PREP_TILE)
    out_shape = [jax.ShapeDtypeStruct((b, nh, s, 2 * LANES), MXU_DTYPE),
                 jax.ShapeDtypeStruct((b, nh, s, 2 * LANES), MXU_DTYPE),
                 jax.ShapeDtypeStruct((b, nh, s, LANES), MXU_DTYPE)]
    out_specs = [pl.BlockSpec((None, nh, PREP_TILE, 2 * LANES), lambda i, j: (i, 0, j, 0)),
                 pl.BlockSpec((None, nh, PREP_TILE, 2 * LANES), lambda i, j: (i, 0, j, 0)),
                 pl.BlockSpec((None, nh, PREP_TILE, LANES), lambda i, j: (i, 0, j, 0))]

    def col(width, offset):
        return pl.BlockSpec((None, PREP_TILE, width), lambda i, j: (i, j, offset // width))

    const2 = lambda i, j: (0, 0)
    if mode == "fox":
        return pl.pallas_call(
            _fox_prep_kernel, grid=grid,
            in_specs=[col(GW, C_FX_Q), col(GW, C_FX_K), col(GW, C_FX_V), col(LANES, C_FX_G),
                      pl.BlockSpec((1, LANES), const2), pl.BlockSpec((PREP_TILE, PREP_TILE), const2)],
            out_specs=out_specs, out_shape=out_shape,
            scratch_shapes=[pltpu.VMEM((8, LANES), F32)],
            compiler_params=_params(2), name="fox_prep",
        )(p3, p3, p3, p3, gate_bias, tri)
    return pl.pallas_call(
        _mlstm_prep_kernel, grid=grid,
        in_specs=[col(2 * GW, C_ML_Q), col(GW, C_ML_V), col(LANES, C_ML_G),
                  pl.BlockSpec((1, LANES), const2), pl.BlockSpec((8, 2 * GW), const2),
                  pl.BlockSpec((PREP_TILE, PREP_TILE), const2)],
        out_specs=out_specs, out_shape=out_shape,
        scratch_shapes=[pltpu.VMEM((8, LANES), F32), pltpu.VMEM((PREP_TILE + 8, 2 * GW), F32)],
        compiler_params=_params(2), name="mlstm_prep",
    )(p3, p3, p3, gate_bias, conv_w, tri)


def _fox_flash_kernel(qa_ref, ka_ref, v_ref, out_ref):
    i = pl.program_id(2)
    tq = FLASH_TILE
    qa = qa_ref[...]

    def update(carry, s, v):
        m, l, acc = carry
        m_new = jnp.maximum(m, jnp.max(s, axis=1, keepdims=True))
        alpha = jnp.exp(m - m_new)
        p = jnp.exp(s - m_new)
        l = alpha * l + jnp.sum(p, axis=1, keepdims=True)
        acc = alpha * acc + _dot(_mx(p), v)
        return m_new, l, acc

    def body(j, carry):
        off = pl.multiple_of(j * tq, tq)
        s = _dot_nt(qa, ka_ref[pl.ds(off, tq), :])
        return update(carry, s, v_ref[pl.ds(off, tq), :])

    init = (jnp.full((tq, 1), NEG_BIG, F32), jnp.zeros((tq, 1), F32), jnp.zeros((tq, FX_DH), F32))
    carry = lax.fori_loop(0, i, body, init)
    off = pl.multiple_of(i * tq, tq)
    s = _dot_nt(qa, ka_ref[pl.ds(off, tq), :])
    row = lax.broadcasted_iota(jnp.int32, (tq, tq), 0)
    colv = lax.broadcasted_iota(jnp.int32, (tq, tq), 1)
    s = jnp.where(colv <= row, s, NEG_BIG)
    _, l, acc = update(carry, s, v_ref[pl.ds(off, tq), :])
    out_ref[...] = _mx(acc / l)


def _mlstm_flash_kernel(qa_ref, ka_ref, v_ref, o_ref, ng_ref, out_ref):
    i = pl.program_id(2)
    tq = FLASH_TILE
    qa = qa_ref[...]
    q = qa[:, :ML_DH]
    qb = qa[:, ML_DH:]

    def update(carry, qk, d, v):
        m, l, acc = carry
        m_new = jnp.maximum(m, jnp.max(d, axis=1, keepdims=True))
        alpha = jnp.exp(m - m_new)
        p = jnp.exp(d - m_new) * qk
        l = alpha * l + jnp.sum(p, axis=1, keepdims=True)
        acc = alpha * acc + _dot(_mx(p), v)
        return m_new, l, acc

    def tiles(off):
        ka = ka_ref[pl.ds(off, tq), :]
        return _dot_nt(q, ka[:, :ML_DH]), _dot_nt(qb, ka[:, ML_DH:]), v_ref[pl.ds(off, tq), :]

    def body(j, carry):
        qk, d, v = tiles(pl.multiple_of(j * tq, tq))
        return update(carry, qk, d, v)

    lane = lax.broadcasted_iota(jnp.int32, (tq, LANES), 1)
    f_row = jnp.sum(jnp.where(lane < 3, qb.astype(F32), 0.0), axis=1, keepdims=True)
    init = (f_row, jnp.zeros((tq, 1), F32), jnp.zeros((tq, ML_DH), F32))
    carry = lax.fori_loop(0, i, body, init)
    qk, d, v = tiles(pl.multiple_of(i * tq, tq))
    row = lax.broadcasted_iota(jnp.int32, (tq, tq), 0)
    colv = lax.broadcasted_iota(jnp.int32, (tq, tq), 1)
    d = jnp.where(colv <= row, d, NEG_BIG)
    m, l, acc = update(carry, qk, d, v)
    h = acc / jnp.maximum(jnp.abs(l), jnp.exp(-m))
    h = h * lax.rsqrt(jnp.mean(h * h, axis=1, keepdims=True) + NORM_EPS) * ng_ref[...]
    out_ref[...] = _mx(_sigmoid(o_ref[...]) * h)


def _flash(mode, qa, ka, vb, p3, norm_g):
    b, nh, s, _ = qa.shape
    tq = FLASH_TILE
    grid = (b, nh, s // tq)
    qspec = pl.BlockSpec((None, None, tq, 2 * LANES), lambda bi, h, i: (bi, h, i, 0))
    kspec = pl.BlockSpec((None, None, s, 2 * LANES), lambda bi, h, i: (bi, h, 0, 0))
    vspec = pl.BlockSpec((None, None, s, LANES), lambda bi, h, i: (bi, h, 0, 0))
    ospec = pl.BlockSpec((None, tq, LANES), lambda bi, h, i: (bi, i, h))
    oshape = jax.ShapeDtypeStruct((b, s, GW), MXU_DTYPE)
    if mode == "fox":
        return pl.pallas_call(
            _fox_flash_kernel, grid=grid, in_specs=[qspec, kspec, vspec], out_specs=ospec, out_shape=oshape,
            compiler_params=_params(3), name="fox_flash",
        )(qa, ka, vb)
    gate = pl.BlockSpec((None, tq, LANES), lambda bi, h, i: (bi, i, C_ML_O // LANES + h))
    ng = pl.BlockSpec((1, LANES), lambda bi, h, i: (0, h))
    return pl.pallas_call(
        _mlstm_flash_kernel, grid=grid, in_specs=[qspec, kspec, vspec, gate, ng],
        out_specs=ospec, out_shape=oshape, compiler_params=_params(3), name="mlstm_flash",
    )(qa, ka, vb, p3, norm_g)


def _hgrn2_kernel(q_ref, z_ref, i_ref, g_ref, lb_ref, ng_ref, mats_ref, out_ref, state_ref):
    L = HG_CHUNK

    @pl.when(pl.program_id(2) == 0)
    def _():
        state_ref[...] = jnp.zeros_like(state_ref)

    lb = lb_ref[...]
    z = z_ref[...]
    q = _silu(q_ref[...])
    v = _silu(i_ref[...])
    lf = _log_sigmoid(z) + jnp.log1p(lb * jnp.exp(jnp.minimum(-z, EXP_CLIP)))
    k = (1.0 - lb) * _sigmoid(-z)
    sums = _select_matmul(mats_ref[...], lf)
    b = sums[0:L]
    sfx = sums[L:2 * L]
    row = lax.broadcasted_iota(jnp.int32, (L, L), 0)
    colv = lax.broadcasted_iota(jnp.int32, (L, L), 1)
    diff = row ^ colv
    att = jnp.zeros((L, L), F32)
    for lev in range(HG_LEVELS):
        dq = sums[(2 + lev) * L:(3 + lev) * L]
        dk = sums[(2 + HG_LEVELS + lev) * L:(3 + HG_LEVELS + lev) * L]
        p = _dot_nt(_mx(q * jnp.exp(dq)), _mx(k * jnp.exp(dk)))
        att = jnp.where(((diff >> lev) == 1) & (row > colv), p, att)
    vb = _mx(v)
    st = state_ref[...]
    o = _dot(_mx(att), vb) + jnp.sum(q * k, axis=1, keepdims=True) * v
    o = o + _dot_nt(_mx(q * jnp.exp(b)), _mx(st))
    b_last = b[L - 1:L, :]
    state_ref[...] = st * jnp.exp(b_last) + _dot_tn(vb, _mx(k * jnp.exp(sfx)))
    o = o * _sigmoid(g_ref[...])
    out_ref[...] = _mx(o * lax.rsqrt(jnp.mean(o * o, axis=1, keepdims=True) + NORM_EPS) * ng_ref[...])


def _hgrn2_mats():
    L = HG_CHUNK
    t = np.arange(L)[:, None]
    j = np.arange(L)[None, :]
    blocks = [j <= t, j > t]
    for lev in range(HG_LEVELS):
        start = (t >> lev) << lev
        blocks.append((j > start) & (j <= t))
    for lev in range(HG_LEVELS):
        nxt = ((t >> lev) + 1) << lev
        blocks.append((j > t) & (j <= nxt))
    return jnp.asarray(np.concatenate(blocks, axis=0).astype(np.float32), dtype=MXU_DTYPE)


def _hgrn2(p3, lb, norm_g):
    b, s, _ = p3.shape
    L = HG_CHUNK

    def col(offset):
        return pl.BlockSpec((None, L, LANES), lambda bi, h, c: (bi, c, offset // LANES + h))

    hvec = pl.BlockSpec((1, LANES), lambda bi, h, c: (0, h))
    mats = _hgrn2_mats()
    return pl.pallas_call(
        _hgrn2_kernel, grid=(b, HG_HEADS, s // L),
        in_specs=[col(C_HG_Q), col(C_HG_Z), col(C_HG_I), col(C_HG_G), hvec, hvec,
                  pl.BlockSpec(mats.shape, lambda bi, h, c: (0, 0))],
        out_specs=pl.BlockSpec((None, L, LANES), lambda bi, h, c: (bi, c, h)),
        out_shape=jax.ShapeDtypeStruct((b, s, GW), MXU_DTYPE),
        scratch_shapes=[pltpu.VMEM((LANES, LANES), F32)],
        compiler_params=_params(3), name="hgrn2",
    )(p3, p3, p3, p3, lb, norm_g, mats)


def _rwkv_kernel(rkv_ref, gd_ref, wa_ref, mu_rkv, mu_gd, mu_wa, w0_ref, a0_ref, kk_ref, ka_ref, rk_ref,
                 gng_ref, gnb_ref, w2_ref, a2_ref, g2_ref, tri_ref, out_ref,
                 sh_rkv, sh_gd, sh_wa, state_ref):
    L = RW_CHUNK
    first = pl.program_id(1) == 0

    @pl.when(first)
    def _():
        state_ref[...] = jnp.zeros_like(state_ref)

    def lerp(x_ref, sh_ref, mu_ref):
        x = x_ref[...]
        sh = _shift_rows(x, sh_ref, first)
        return x + (sh[7:7 + L, :] - x) * mu_ref[...]

    rkv = lerp(rkv_ref, sh_rkv, mu_rkv)
    gd = lerp(gd_ref, sh_gd, mu_gd)
    wa = lerp(wa_ref, sh_wa, mu_wa)
    r_all, k_all, v_all = rkv[:, :RW_W], rkv[:, RW_W:2 * RW_W], rkv[:, 2 * RW_W:]
    wd, ad = wa[:, :LANES], wa[:, LANES:]

    wpre = w0_ref[...] + _dot(_mx(jnp.tanh(wd)), w2_ref[...])
    wlog_all = -jnp.exp(-_softplus(-wpre) - 0.5)
    a_all = _sigmoid(a0_ref[...] + _dot(_mx(ad), a2_ref[...]))
    g_all = _dot(_mx(_sigmoid(gd)), g2_ref[...])

    tri = tri_ref[...]
    row = lax.broadcasted_iota(jnp.int32, (L, L), 0)
    colv = lax.broadcasted_iota(jnp.int32, (L, L), 1)
    diff = row ^ colv
    strict = row > colv
    causal = row >= colv
    eye = row == colv
    lane = lax.broadcasted_iota(jnp.int32, (L, RW_PAD), 1)
    valid = lane < RW_DH
    mid = L // 2

    heads = range(RW_HEADS)
    sls = [slice(h * RW_PAD, (h + 1) * RW_PAD) for h in heads]
    hd = []
    for sl in sls:
        r, k, v, a, wlog = r_all[:, sl], k_all[:, sl], v_all[:, sl], a_all[:, sl], wlog_all[:, sl]
        kk = k * kk_ref[:, sl]
        kk = kk * lax.rsqrt(jnp.maximum(jnp.sum(kk * kk, axis=1, keepdims=True), 1e-24))
        hd.append(dict(r=r, v=v, vb=_mx(v), kk=kk, beta=kk * a, kt=k * (1.0 + (a - 1.0) * ka_ref[:, sl]),
                       wlog=wlog))
    for d in hd:
        d["b"] = _select_matmul(tri, d["wlog"])
    for d in hd:
        b = d["b"]
        bp = b - d["wlog"]
        bm = b[mid:mid + 1, :]
        bl = b[L - 1:L, :]
        e_out = jnp.exp(bm - b)
        e_end = jnp.exp(bl - b)
        d["lhs"] = _mx(jnp.concatenate([d["kk"] * jnp.exp(bp - bm), d["r"] * jnp.exp(b - bm)], axis=0))
        d["rhs_b"] = _mx(d["beta"] * e_out)
        d["rhs_k"] = _mx(d["kt"] * e_out)
        d["kd"] = d["kk"] * jnp.exp(bp)
        d["rd"] = d["r"] * jnp.exp(b)
        d["kte"] = _mx(d["kt"] * e_end)
        d["be"] = _mx(d["beta"] * e_end)
        d["wl"] = jnp.exp(bl)
    for d in hd:
        pb = _dot_nt(d["lhs"], d["rhs_b"])
        pk = _dot_nt(d["lhs"], d["rhs_k"])
        d["amat"] = jnp.where(strict, pb[:L], 0.0)
        d["dmat"] = _mx(jnp.where(causal, pb[L:], 0.0))
        d["bmat"] = _mx(jnp.where(strict, pk[:L], 0.0))
        d["cmat"] = _mx(jnp.where(causal, pk[L:], 0.0))
        d["tinv"] = jnp.where(eye, 1.0, 0.0) - jnp.where(diff == 1, d["amat"], 0.0)
    for lev in range(1, RW_LEVELS):
        lev_mask = (diff >> lev) == 1
        for d in hd:
            d["ta"] = _dot(_mx(d["tinv"]), _mx(jnp.where(lev_mask, d["amat"], 0.0)))
        for d in hd:
            d["tinv"] = d["tinv"] - _dot(_mx(d["ta"]), _mx(d["tinv"]))
    for d in hd:
        d["bv"] = _dot(d["bmat"], d["vb"])
        d["cv"] = _dot(d["cmat"], d["vb"])
    for d in hd:
        d["uk"] = _dot(_mx(d["tinv"]), _mx(jnp.concatenate([d["bv"], d["kd"]], axis=1)))
    for d in hd:
        ukb = _mx(d["uk"])
        d["yr"] = jnp.concatenate([d["cv"], d["rd"]], axis=1) - _dot(d["dmat"], ukb)
        d["ub"], d["kb"] = ukb[:, :RW_PAD], ukb[:, RW_PAD:]
    for d in hd:
        d["gmat"] = jnp.where(eye, d["wl"], 0.0) - _dot_tn(d["kb"], d["be"])
        d["hmat"] = _dot_tn(d["vb"], d["kte"]) - _dot_tn(d["ub"], d["be"])
    for h, d in enumerate(hd):
        st = state_ref[h]
        d["y"] = _dot_nt(_mx(d["yr"][:, RW_PAD:]), _mx(st)) + d["yr"][:, :RW_PAD]
        state_ref[h] = _dot_hi(st, d["gmat"]) + d["hmat"]
    outs = []
    for sl, d in zip(sls, hd):
        y = d["y"]
        mu_y = jnp.sum(y, axis=1, keepdims=True) * (1.0 / RW_DH)
        dy = jnp.where(valid, y - mu_y, 0.0)
        var_y = jnp.sum(dy * dy, axis=1, keepdims=True) * (1.0 / RW_DH)
        yn = dy * lax.rsqrt(var_y + RW_GN_EPS) * gng_ref[:, sl] + gnb_ref[:, sl]
        bonus = jnp.sum(d["r"] * d["kt"] * rk_ref[:, sl], axis=1, keepdims=True) * d["v"]
        outs.append((yn + bonus) * g_all[:, sl])
    out_ref[...] = _mx(jnp.concatenate(outs, axis=1))


def _rwkv(p3, rp, tri):
    b, s, _ = p3.shape
    L = RW_CHUNK
    c2 = lambda bi, c: (0, 0)

    def vec(width):
        return pl.BlockSpec((1, width), c2)

    return pl.pallas_call(
        _rwkv_kernel, grid=(b, s // L),
        in_specs=[pl.BlockSpec((None, L, 3 * RW_W), lambda bi, c: (bi, c, 0)),
                  pl.BlockSpec((None, L, 2 * LANES), lambda bi, c: (bi, c, C_RW_GD // (2 * LANES))),
                  pl.BlockSpec((None, L, 2 * LANES), lambda bi, c: (bi, c, C_RW_WD // (2 * LANES))),
                  vec(3 * RW_W), vec(2 * LANES), vec(2 * LANES),
                  vec(RW_W), vec(RW_W), vec(RW_W), vec(RW_W), vec(RW_W), vec(RW_W), vec(RW_W),
                  pl.BlockSpec((LANES, RW_W), c2), pl.BlockSpec((LANES, RW_W), c2),
                  pl.BlockSpec((2 * LANES, RW_W), c2), pl.BlockSpec((L, L), c2)],
        out_specs=pl.BlockSpec((None, L, RW_W), lambda bi, c: (bi, c, 0)),
        out_shape=jax.ShapeDtypeStruct((b, s, RW_W), MXU_DTYPE),
        scratch_shapes=[pltpu.VMEM((L + 8, 3 * RW_W), F32), pltpu.VMEM((L + 8, 2 * LANES), F32),
                        pltpu.VMEM((L + 8, 2 * LANES), F32), pltpu.VMEM((RW_HEADS, RW_PAD, RW_PAD), F32)],
        compiler_params=_params(2), name="rwkv7",
    )(p3, p3, p3, rp["mu_rkv"], rp["mu_gd"], rp["mu_wa"], rp["w0"], rp["a0"], rp["k_k"], rp["k_a"], rp["r_k"],
      rp["gn_g"], rp["gn_b"], rp["w2"], rp["a2"], rp["g2"], tri)


def _outproj_kernel(yml_ref, yrw_ref, yfx_ref, yhg_ref, wml_ref, wrw_ref, wfx_ref, whg_ref,
                    x_ref, g_ref, b_ref, o_ref):
    mix = _dot(yml_ref[...], wml_ref[...]) + _dot(yrw_ref[...], wrw_ref[...])
    mix = mix + _dot(yfx_ref[...], wfx_ref[...]) + _dot(yhg_ref[...], whg_ref[...])
    o_ref[...] = _layer_norm(ALPHA * x_ref[...] + mix, g_ref[...], b_ref[...])


def _outproj(y_ml, y_rw, y_fx, y_hg, wo, x2d, g, bvec):
    t = x2d.shape[0]
    tm = ROW_TILE // 2
    row = lambda w: pl.BlockSpec((tm, w), lambda i: (i, 0))
    full = lambda a: pl.BlockSpec(a.shape, lambda i: (0, 0))
    return pl.pallas_call(
        _outproj_kernel, grid=(t // tm,),
        in_specs=[row(GW), row(RW_W), row(GW), row(GW), full(wo["ml"]), full(wo["rw"]), full(wo["fx"]),
                  full(wo["hg"]), row(D_MODEL), full(g), full(bvec)],
        out_specs=row(D_MODEL), out_shape=jax.ShapeDtypeStruct((t, D_MODEL), F32),
        compiler_params=_params(1), name="outproj_ln",
    )(y_ml, y_rw, y_fx, y_hg, wo["ml"], wo["rw"], wo["fx"], wo["hg"], x2d, g, bvec)


def _ffn_kernel(x_ref, wg_ref, wu_ref, wd_ref, g_ref, b_ref, o_ref, xb_ref, acc_ref):
    j = pl.program_id(1)

    @pl.when(j == 0)
    def _():
        xb_ref[...] = _mx(x_ref[...])
        acc_ref[...] = jnp.zeros_like(acc_ref)

    xb = xb_ref[...]
    h = _silu(_dot(xb, wg_ref[...])) * _dot(xb, wu_ref[...])
    acc_ref[...] += _dot(_mx(h), wd_ref[...])

    @pl.when(j == pl.num_programs(1) - 1)
    def _():
        o_ref[...] = _layer_norm(ALPHA * x_ref[...] + acc_ref[...], g_ref[...], b_ref[...])


def _ffn(x2d, wg, wu, wd, g, bvec):
    t = x2d.shape[0]
    tm, tf = ROW_TILE, FF_TILE
    vec = pl.BlockSpec((1, D_MODEL), lambda i, j: (0, 0))
    return pl.pallas_call(
        _ffn_kernel, grid=(t // tm, D_FF_PAD // tf),
        in_specs=[pl.BlockSpec((tm, D_MODEL), lambda i, j: (i, 0)),
                  pl.BlockSpec((D_MODEL, tf), lambda i, j: (0, j)),
                  pl.BlockSpec((D_MODEL, tf), lambda i, j: (0, j)),
                  pl.BlockSpec((tf, D_MODEL), lambda i, j: (j, 0)), vec, vec],
        out_specs=pl.BlockSpec((tm, D_MODEL), lambda i, j: (i, 0)),
        out_shape=jax.ShapeDtypeStruct((t, D_MODEL), F32),
        scratch_shapes=[pltpu.VMEM((tm, D_MODEL), MXU_DTYPE), pltpu.VMEM((tm, D_MODEL), F32)],
        compiler_params=_params(2), name="ffn_ln",
    )(x2d, wg, wu, wd, g, bvec)


def _router_kernel(x_ref, w_ref, idx_ref, wt_ref):
    logits = _dot_hi(x_ref[...], w_ref[...])
    lane = lax.broadcasted_iota(jnp.int32, logits.shape, 1)
    valid = lane < N_EXPERTS
    logits = jnp.where(valid, logits, NEG_BIG)
    e = jnp.exp(logits - jnp.max(logits, axis=1, keepdims=True))
    probs = jnp.where(valid, e / jnp.sum(e, axis=1, keepdims=True), -1.0)
    p1 = jnp.max(probs, axis=1, keepdims=True)
    i1 = jnp.min(jnp.where(probs == p1, lane, LANES), axis=1, keepdims=True)
    rest = jnp.where(lane == i1, -1.0, probs)
    p2 = jnp.max(rest, axis=1, keepdims=True)
    i2 = jnp.min(jnp.where(rest == p2, lane, LANES), axis=1, keepdims=True)
    tot = p1 + p2
    idx_ref[...] = jnp.where(lane == 0, i1, jnp.where(lane == 1, i2, 0))
    wt_ref[...] = jnp.where(lane == 0, p1 / tot, jnp.where(lane == 1, p2 / tot, 0.0))


def _router(x2d, w_router):
    t = x2d.shape[0]
    tm = ROW_TILE
    out = pl.BlockSpec((tm, LANES), lambda i: (i, 0))
    return pl.pallas_call(
        _router_kernel, grid=(t // tm,),
        in_specs=[pl.BlockSpec((tm, D_MODEL), lambda i: (i, 0)), pl.BlockSpec((D_MODEL, LANES), lambda i: (0, 0))],
        out_specs=[out, out],
        out_shape=[jax.ShapeDtypeStruct((t, LANES), jnp.int32), jax.ShapeDtypeStruct((t, LANES), F32)],
        compiler_params=_params(1), name="moe_router",
    )(x2d, w_router)


def _route(idx2, n_slots):
    t = idx2.shape[0]
    e_flat = idx2.reshape(-1)
    onehot = (e_flat[:, None] == jnp.arange(N_EXPERTS, dtype=jnp.int32)[None, :]).astype(jnp.int32)
    csum = jnp.cumsum(onehot, axis=0)
    rank = jnp.sum(onehot * csum, axis=1) - 1
    counts = csum[-1]
    tiles_per = (counts + ROW_TILE - 1) // ROW_TILE
    tile_end = jnp.cumsum(tiles_per)
    tile_start = tile_end - tiles_per
    pos = jnp.sum(onehot * tile_start[None, :], axis=1) * ROW_TILE + rank
    tok = jnp.zeros((n_slots,), jnp.int32).at[pos].set(jnp.arange(2 * t, dtype=jnp.int32) // 2)
    n_tiles = n_slots // ROW_TILE
    tile_ids = jnp.arange(n_tiles, dtype=jnp.int32)
    tile_expert = jnp.sum((tile_ids[:, None] >= tile_end[None, :]).astype(jnp.int32), axis=1)
    return tok, pos, jnp.minimum(tile_expert, N_EXPERTS - 1)


def _row_copy(src_hbm, src_row, dst_ref, dst_row, sem):
    return pltpu.make_async_copy(src_hbm.at[pl.ds(src_row, 1)], dst_ref.at[pl.ds(dst_row, 1)], sem)


def _gather_kernel(tok_ref, x_hbm, o_hbm, sem):
    base = pl.program_id(0) * GATHER_ROWS

    def issue(r, c):
        _row_copy(x_hbm, tok_ref[r], o_hbm, base + r, sem).start()
        return c

    def drain(r, c):
        _row_copy(x_hbm, tok_ref[r], o_hbm, base + r, sem).wait()
        return c

    lax.fori_loop(0, GATHER_ROWS, issue, 0)
    lax.fori_loop(0, GATHER_ROWS, drain, 0)


def _gather_rows(tok, x2d):
    n = tok.shape[0]
    return pl.pallas_call(
        _gather_kernel, grid=(n // GATHER_ROWS,),
        in_specs=[pl.BlockSpec((GATHER_ROWS,), lambda i: (i,), memory_space=pltpu.SMEM),
                  pl.BlockSpec(memory_space=pl.ANY)],
        out_specs=pl.BlockSpec(memory_space=pl.ANY),
        out_shape=jax.ShapeDtypeStruct((n, x2d.shape[1]), x2d.dtype),
        scratch_shapes=[pltpu.SemaphoreType.DMA],
        compiler_params=_params(1), name="moe_gather",
    )(tok, x2d)


def _expert_ffn_kernel(te_ref, x_ref, wg_ref, wu_ref, wd_ref, o_ref, xb_ref, acc_ref):
    j = pl.program_id(1)

    @pl.when(j == 0)
    def _():
        xb_ref[...] = _mx(x_ref[...])
        acc_ref[...] = jnp.zeros_like(acc_ref)

    xb = xb_ref[...]
    h = _silu(_dot(xb, wg_ref[...])) * _dot(xb, wu_ref[...])
    acc_ref[...] += _dot(_mx(h), wd_ref[...])

    @pl.when(j == pl.num_programs(1) - 1)
    def _():
        o_ref[...] = acc_ref[...]


def _expert_ffn(tile_expert, xs, wg, wu, wd):
    n = xs.shape[0]
    tm, tf = ROW_TILE, FF_TILE
    grid_spec = pltpu.PrefetchScalarGridSpec(
        num_scalar_prefetch=1, grid=(n // tm, D_FF_PAD // tf),
        in_specs=[pl.BlockSpec((tm, D_MODEL), lambda i, j, te: (i, 0)),
                  pl.BlockSpec((None, D_MODEL, tf), lambda i, j, te: (te[i], 0, j)),
                  pl.BlockSpec((None, D_MODEL, tf), lambda i, j, te: (te[i], 0, j)),
                  pl.BlockSpec((None, tf, D_MODEL), lambda i, j, te: (te[i], j, 0))],
        out_specs=pl.BlockSpec((tm, D_MODEL), lambda i, j, te: (i, 0)),
        scratch_shapes=[pltpu.VMEM((tm, D_MODEL), MXU_DTYPE), pltpu.VMEM((tm, D_MODEL), F32)])
    return pl.pallas_call(
        _expert_ffn_kernel, grid_spec=grid_spec,
        out_shape=jax.ShapeDtypeStruct((n, D_MODEL), F32),
        compiler_params=_params(2), name="moe_expert_ffn",
    )(tile_expert, xs, wg, wu, wd)


def _combine_kernel(pos_ref, x_ref, wt_ref, g_ref, b_ref, ys_hbm, o_ref, buf_ref, sem):
    tm = COMBINE_ROWS

    def issue(r, c):
        _row_copy(ys_hbm, pos_ref[2 * r], buf_ref.at[0], r, sem).start()
        _row_copy(ys_hbm, pos_ref[2 * r + 1], buf_ref.at[1], r, sem).start()
        return c

    def drain(r, c):
        _row_copy(ys_hbm, pos_ref[2 * r], buf_ref.at[0], r, sem).wait()
        _row_copy(ys_hbm, pos_ref[2 * r + 1], buf_ref.at[1], r, sem).wait()
        return c

    lax.fori_loop(0, tm, issue, 0)
    lax.fori_loop(0, tm, drain, 0)
    wt = wt_ref[...]
    ff = _lane_pick(wt, 0) * buf_ref[0] + _lane_pick(wt, 1) * buf_ref[1]
    o_ref[...] = _layer_norm(ALPHA * x_ref[...] + ff, g_ref[...], b_ref[...])


def _moe_combine(pos, ys, x2d, wts, g, bvec):
    t = x2d.shape[0]
    tm = COMBINE_ROWS
    vec = pl.BlockSpec((1, D_MODEL), lambda i: (0, 0))
    return pl.pallas_call(
        _combine_kernel, grid=(t // tm,),
        in_specs=[pl.BlockSpec((2 * tm,), lambda i: (i,), memory_space=pltpu.SMEM),
                  pl.BlockSpec((tm, D_MODEL), lambda i: (i, 0)),
                  pl.BlockSpec((tm, LANES), lambda i: (i, 0)), vec, vec,
                  pl.BlockSpec(memory_space=pl.ANY)],
        out_specs=pl.BlockSpec((tm, D_MODEL), lambda i: (i, 0)),
        out_shape=jax.ShapeDtypeStruct((t, D_MODEL), F32),
        scratch_shapes=[pltpu.VMEM((2, tm, D_MODEL), F32), pltpu.SemaphoreType.DMA],
        compiler_params=_params(1), name="moe_combine_ln",
    )(pos, x2d, wts, g, bvec, ys)


def _pad_cols(a, width):
    return jnp.pad(a, ((0, 0), (0, width - a.shape[1])))


def _pad_heads(a):
    lead = a.shape[:-1]
    a = a.reshape(lead + (RW_HEADS, RW_DH))
    a = jnp.pad(a, [(0, 0)] * len(lead) + [(0, 0), (0, RW_PAD - RW_DH)])
    return a.reshape(lead + (RW_W,))


def _pack_w_in(w):
    ml_n = 4 * GW + 2 * ML_HEADS
    rw_n = 3 * GW + RW_DECAY_LORA + RW_A_LORA + RW_GATE_LORA
    fx_n = 3 * GW + FX_HEADS
    ml, rw, fx, hg = w[:, :ml_n], w[:, ml_n:ml_n + rw_n], w[:, ml_n + rw_n:ml_n + rw_n + fx_n], w[:, ml_n + rw_n + fx_n:]
    o = 3 * GW
    pieces = [_pad_heads(rw[:, :GW]), _pad_heads(rw[:, GW:2 * GW]), _pad_heads(rw[:, 2 * GW:3 * GW]),
              ml[:, :4 * GW], fx[:, :3 * GW], hg,
              rw[:, o + RW_DECAY_LORA + RW_A_LORA:],
              _pad_cols(ml[:, 4 * GW:], LANES), _pad_cols(fx[:, 3 * GW:], LANES),
              _pad_cols(rw[:, o:o + RW_DECAY_LORA], LANES),
              _pad_cols(rw[:, o + RW_DECAY_LORA:o + RW_DECAY_LORA + RW_A_LORA], LANES)]
    return jnp.concatenate(pieces, axis=1).astype(MXU_DTYPE)


def _row(a, width=None):
    a = a.reshape(1, -1).astype(F32)
    return a if width is None else _pad_cols(a, width)


def _pad_rows(a, rows):
    return jnp.pad(a, ((0, rows - a.shape[0]), (0, 0)))


def _rwkv_params(mu, w0, w2, a0, a2, g2, k_k, k_a, r_k, gn_g, gn_b):
    o = 3 * GW
    mu_rkv = jnp.concatenate([_pad_heads(mu[:GW]), _pad_heads(mu[GW:2 * GW]), _pad_heads(mu[2 * GW:o])])
    mu_wd = jnp.pad(mu[o:o + RW_DECAY_LORA], (0, LANES - RW_DECAY_LORA))
    mu_ad = jnp.pad(mu[o + RW_DECAY_LORA:o + RW_DECAY_LORA + RW_A_LORA], (0, LANES - RW_A_LORA))
    return {
        "mu_rkv": _row(mu_rkv), "mu_gd": _row(mu[o + RW_DECAY_LORA + RW_A_LORA:]),
        "mu_wa": _row(jnp.concatenate([mu_wd, mu_ad])),
        "w0": _row(_pad_heads(w0)), "a0": _row(_pad_heads(a0)), "k_k": _row(_pad_heads(k_k)),
        "k_a": _row(_pad_heads(k_a)), "r_k": _row(_pad_heads(r_k.reshape(-1))),
        "gn_g": _row(_pad_heads(gn_g)), "gn_b": _row(_pad_heads(gn_b)),
        "w2": _pad_rows(_pad_heads(w2), LANES).astype(MXU_DTYPE),
        "a2": _pad_rows(_pad_heads(a2), LANES).astype(MXU_DTYPE),
        "g2": _pad_heads(g2).astype(MXU_DTYPE),
    }


def _pack_w_out(w):
    rw = w[GW:2 * GW].reshape(RW_HEADS, RW_DH, D_MODEL)
    rw = jnp.pad(rw, ((0, 0), (0, RW_PAD - RW_DH), (0, 0))).reshape(RW_W, D_MODEL)
    return {"ml": w[:GW].astype(MXU_DTYPE), "rw": rw.astype(MXU_DTYPE),
            "fx": w[2 * GW:3 * GW].astype(MXU_DTYPE), "hg": w[3 * GW:].astype(MXU_DTYPE)}


def _pad_ff(wg, wu, wd):
    padc = [(0, 0)] * (wg.ndim - 1) + [(0, D_FF_PAD - D_FF)]
    padr = [(0, 0)] * (wd.ndim - 2) + [(0, D_FF_PAD - D_FF), (0, 0)]
    return (jnp.pad(wg, padc).astype(MXU_DTYPE), jnp.pad(wu, padc).astype(MXU_DTYPE),
            jnp.pad(wd, padr).astype(MXU_DTYPE))


def _tri(n):
    return jnp.asarray(np.tril(np.ones((n, n), np.float32)), dtype=MXU_DTYPE)


def kernel(x, w_in, w_out, ln1_g, ln1_b, ln2_g, ln2_b, ml_conv, ml_b_i, ml_b_f, ml_norm_g, rw_mu, rw_w0, rw_w2, rw_a0, rw_a2, rw_g2, rw_k_k, rw_k_a, rw_r_k, rw_gn_g, rw_gn_b, fx_b_f, hg_lb_logits, hg_norm_g, ffn_w_gate, ffn_w_up, ffn_w_down, moe_router, moe_w_gate, moe_w_up, moe_w_down):
    bsz, seq, d = x.shape
    depth = w_in.shape[0]
    assert d == D_MODEL and seq % FLASH_TILE == 0 and (bsz * seq) % ROW_TILE == 0
    lb_probs = jax.nn.softmax(hg_lb_logits.astype(F32), axis=0)
    lower_bounds = jnp.cumsum(lb_probs, axis=0) - lb_probs[0]
    tri_prep = _tri(PREP_TILE)
    tri_rw = _tri(RW_CHUNK)
    x2d = x.reshape(bsz * seq, d)
    for l in range(depth):
        p3 = _inproj(x2d, _pack_w_in(w_in[l])).reshape(bsz, seq, N_PACK)

        ml_gate_bias = _row(jnp.concatenate([ml_b_i[l], ml_b_f[l]]), LANES)
        qa, ka, vb = _flash_prep("mlstm", p3, ml_gate_bias, _pad_rows(ml_conv[l], 8), tri_prep)
        y_ml = _flash("mlstm", qa, ka, vb, p3, _row(ml_norm_g[l]))

        qa, ka, vb = _flash_prep("fox", p3, _row(fx_b_f[l], LANES), None, tri_prep)
        y_fx = _flash("fox", qa, ka, vb, None, None)

        y_hg = _hgrn2(p3, _row(lower_bounds[l]), _row(hg_norm_g[l]))

        rp = _rwkv_params(rw_mu[l], rw_w0[l], rw_w2[l], rw_a0[l], rw_a2[l], rw_g2[l], rw_k_k[l], rw_k_a[l],
                          rw_r_k[l], rw_gn_g[l], rw_gn_b[l])
        y_rw = _rwkv(p3, rp, tri_rw)

        t = bsz * seq
        x2d = _outproj(y_ml.reshape(t, GW), y_rw.reshape(t, RW_W), y_fx.reshape(t, GW), y_hg.reshape(t, GW),
                       _pack_w_out(w_out[l]), x2d, _row(ln1_g[l]), _row(ln1_b[l]))
        j = l // 2
        if l % 2 == 0:
            wg, wu, wd = _pad_ff(ffn_w_gate[j], ffn_w_up[j], ffn_w_down[j])
            x2d = _ffn(x2d, wg, wu, wd, _row(ln2_g[l]), _row(ln2_b[l]))
        else:
            idx, wts = _router(x2d, _pad_cols(moe_router[j].astype(F32), LANES))
            n_slots = 2 * t + N_EXPERTS * ROW_TILE
            tok, pos, tile_expert = _route(idx[:, :2], n_slots)
            wg, wu, wd = _pad_ff(moe_w_gate[j], moe_w_up[j], moe_w_down[j])
            ys = _expert_ffn(tile_expert, _gather_rows(tok, x2d), wg, wu, wd)
            x2d = _moe_combine(pos, ys, x2d, wts, _row(ln2_g[l]), _row(ln2_b[l]))
    return x2d.reshape(bsz, seq, d)
```

```python
import functools

import jax
import jax.numpy as jnp
import numpy as np
from jax import lax
from jax.experimental import pallas as pl
from jax.experimental.pallas import tpu as pltpu

F32 = jnp.float32
MXU_DTYPE = jnp.bfloat16
HIGHEST = lax.Precision.HIGHEST

D_MODEL = 2048
N_GROUPS = 4
GW = D_MODEL // N_GROUPS
ML_HEADS = 4
ML_DH = GW // ML_HEADS
ML_CONV = 4
RW_DH = 64
RW_HEADS = GW // RW_DH
RW_DECAY_LORA = 96
RW_A_LORA = 96
RW_GATE_LORA = 256
FX_HEADS = 4
FX_DH = GW // FX_HEADS
HG_HEADS = 4
D_FF = 5504
N_EXPERTS = 8
DEPTH = 2
ALPHA = (2 * DEPTH) ** 0.25
LN_EPS = 1e-5
NORM_EPS = 1e-6
RW_GN_EPS = 64e-5
NEG_BIG = -1e30
EXP_CLIP = 60.0

LANES = 128
RW_PAD = LANES
RW_W = RW_HEADS * RW_PAD
D_FF_PAD = 5632
FF_TILE = 512
ROW_TILE = 512
PROJ_ROW_TILE = 1024
PROJ_COL_TILE = 512
FLASH_TILE = 512
FLASH_HEADS = 2
PREP_TILE = 512
HG_CHUNK = 128
HG_LEVELS = 7
RW_CHUNK = 128
RW_LEVELS = 7
COMBINE_ROWS = 256
VMEM_LIMIT = 56 * 1024 * 1024

C_RW_R, C_RW_K, C_RW_V = 0, 1024, 2048
C_ML_Q, C_ML_K, C_ML_V, C_ML_O = 3072, 3584, 4096, 4608
C_FX_Q, C_FX_K, C_FX_V = 5120, 5632, 6144
C_HG_Q, C_HG_Z, C_HG_I, C_HG_G = 6656, 7168, 7680, 8192
C_RW_GD = 8704
C_ML_G = 8960
C_FX_G = 9088
C_RW_WD, C_RW_AD = 9216, 9344
N_PACK = 9728
N_IN = 4 * GW + 2 * ML_HEADS + 3 * GW + RW_DECAY_LORA + RW_A_LORA + RW_GATE_LORA + 3 * GW + FX_HEADS + 4 * GW


def _params(n_grid):
    return pltpu.CompilerParams(dimension_semantics=("arbitrary",) * n_grid, vmem_limit_bytes=VMEM_LIMIT)


def _dot(a, b):
    return jnp.dot(a, b, preferred_element_type=F32)


def _dot_nt(a, b):
    return lax.dot_general(a, b, (((1,), (1,)), ((), ())), preferred_element_type=F32)


def _dot_tn(a, b):
    return lax.dot_general(a, b, (((0,), (0,)), ((), ())), preferred_element_type=F32)


def _dot_hi(a, b):
    return jnp.dot(a, b, precision=HIGHEST, preferred_element_type=F32)


def _mx(x):
    return x.astype(MXU_DTYPE)


def _split3(x):
    hi = _mx(x)
    r1 = x - hi.astype(F32)
    mid = _mx(r1)
    lo = _mx(r1 - mid.astype(F32))
    return hi, mid, lo


def _select_matmul(mat, x):
    c = x.shape[1]
    hi, mid, lo = _split3(x)
    r = _dot(mat, jnp.concatenate([hi, mid, lo], axis=1))
    return (r[:, :c] + r[:, c:2 * c]) + r[:, 2 * c:]


def _softplus(x):
    return jnp.maximum(x, 0.0) + jnp.log1p(jnp.exp(-jnp.abs(x)))


def _log_sigmoid(x):
    return -_softplus(-x)


def _sigmoid(x):
    return jax.nn.sigmoid(x)


def _silu(x):
    return x * _sigmoid(x)


def _lane_pick(x, lane):
    idx = lax.broadcasted_iota(jnp.int32, x.shape, 1)
    return jnp.sum(jnp.where(idx == lane, x, 0.0), axis=1, keepdims=True)


def _bias_aug(col, ones_first):
    rows = col.shape[0]
    hi, mid, lo = _split3(col)
    lane = lax.broadcasted_iota(jnp.int32, (rows, LANES), 1)
    p0, p1 = (3, 0) if ones_first else (0, 3)
    out = jnp.where(lane == p0, hi.astype(F32), 0.0)
    out = jnp.where(lane == p0 + 1, mid.astype(F32), out)
    out = jnp.where(lane == p0 + 2, lo.astype(F32), out)
    out = jnp.where((lane >= p1) & (lane < p1 + 3), 1.0, out)
    return _mx(out)


def _shift_rows(x, sh_ref, first_step):
    rows = x.shape[0]

    @pl.when(first_step)
    def _():
        sh_ref[...] = jnp.zeros_like(sh_ref)

    sh_ref[0:8, :] = sh_ref[rows:rows + 8, :]
    sh_ref[8:rows + 8, :] = x
    return sh_ref


def _layer_norm(y, g, b):
    mu = jnp.mean(y, axis=-1, keepdims=True)
    d = y - mu
    var = jnp.mean(d * d, axis=-1, keepdims=True)
    return d * lax.rsqrt(var + LN_EPS) * g + b


def _inproj_kernel(x_ref, w_ref, o_ref, xb_ref):
    @pl.when(pl.program_id(1) == 0)
    def _():
        xb_ref[...] = _mx(x_ref[...])

    o_ref[...] = _dot(xb_ref[...], w_ref[...])


def _inproj(x2d, w_pack):
    t = x2d.shape[0]
    tm = PROJ_ROW_TILE
    return pl.pallas_call(
        _inproj_kernel,
        grid=(t // tm, N_PACK // PROJ_COL_TILE),
        in_specs=[pl.BlockSpec((tm, D_MODEL), lambda i, j: (i, 0)),
                  pl.BlockSpec((D_MODEL, PROJ_COL_TILE), lambda i, j: (0, j))],
        out_specs=pl.BlockSpec((tm, PROJ_COL_TILE), lambda i, j: (i, j)),
        out_shape=jax.ShapeDtypeStruct((t, N_PACK), F32),
        scratch_shapes=[pltpu.VMEM((tm, D_MODEL), MXU_DTYPE)],
        compiler_params=_params(2),
        name="inproj",
    )(x2d, w_pack)


def _fox_prep_kernel(q_ref, k_ref, v_ref, g_ref, bf_ref, tri_ref, qa_ref, ka_ref, vb_ref, carry_ref):
    @pl.when(pl.program_id(1) == 0)
    def _():
        carry_ref[...] = jnp.zeros_like(carry_ref)

    lf = _log_sigmoid(g_ref[...] + bf_ref[...])
    cs = _select_matmul(tri_ref[...], lf) + carry_ref[0:1, :]
    carry_ref[0:1, :] = cs[PREP_TILE - 1:PREP_TILE, :]
    q = q_ref[...] * (FX_DH ** -0.5)
    k = k_ref[...]
    v = v_ref[...]
    for h in range(FX_HEADS):
        sl = slice(h * FX_DH, (h + 1) * FX_DH)
        cf = _lane_pick(cs, h)
        qa_ref[h] = jnp.concatenate([_mx(q[:, sl]), _bias_aug(cf, False)], axis=1)
        ka_ref[h] = jnp.concatenate([_mx(k[:, sl]), _bias_aug(-cf, True)], axis=1)
        vb_ref[h] = _mx(v[:, sl])


def _mlstm_prep_kernel(qk_ref, v_ref, g_ref, gb_ref, conv_ref, tri_ref, qa_ref, ka_ref, vb_ref,
                       carry_ref, sh_ref):
    first = pl.program_id(1) == 0

    @pl.when(first)
    def _():
        carry_ref[...] = jnp.zeros_like(carry_ref)

    gs = g_ref[...] + gb_ref[...]
    lf = _log_sigmoid(gs)
    cs = _select_matmul(tri_ref[...], lf) + carry_ref[0:1, :]
    carry_ref[0:1, :] = cs[PREP_TILE - 1:PREP_TILE, :]

    x = qk_ref[...]
    sh = _shift_rows(x, sh_ref, first)
    acc = conv_ref[0:1, :] * x
    for j in range(1, ML_CONV):
        acc = acc + conv_ref[j:j + 1, :] * sh[8 - j:8 - j + PREP_TILE, :]
    qk = _silu(acc)
    v = v_ref[...]
    for h in range(ML_HEADS):
        sl = slice(h * ML_DH, (h + 1) * ML_DH)
        slk = slice(GW + h * ML_DH, GW + (h + 1) * ML_DH)
        fcol = _lane_pick(cs, ML_HEADS + h)
        icol = _lane_pick(gs, h)
        qa_ref[h] = jnp.concatenate([_mx(qk[:, sl]), _bias_aug(fcol, False)], axis=1)
        ka_ref[h] = jnp.concatenate([_mx(qk[:, slk] * (ML_DH ** -0.5)), _bias_aug(icol - fcol, True)], axis=1)
        vb_ref[h] = _mx(v[:, sl])


def _flash_prep(mode, p3, gate_bias, conv_w, tri):
    b, s, _ = p3.shape
    nh = FX_HEADS
    grid = (b, s // PREP_TILE)
    out_shape = [jax.ShapeDtypeStruct((b, nh, s, 2 * LANES), MXU_DTYPE),
                 jax.ShapeDtypeStruct((b, nh, s, 2 * LANES), MXU_DTYPE),
                 jax.ShapeDtypeStruct((b, nh, s, LANES), MXU_DTYPE)]
    out_specs = [pl.BlockSpec((None, nh, PREP_TILE, 2 * LANES), lambda i, j: (i, 0, j, 0)),
                 pl.BlockSpec((None, nh, PREP_TILE, 2 * LANES), lambda i, j: (i, 0, j, 0)),
                 pl.BlockSpec((None, nh, PREP_TILE, LANES), lambda i, j: (i, 0, j, 0))]

    def col(width, offset):
        return pl.BlockSpec((None, PREP_TILE, width), lambda i, j: (i, j, offset // width))

    const2 = lambda i, j: (0, 0)
    if mode == "fox":
        return pl.pallas_call(
            _fox_prep_kernel, grid=grid,
            in_specs=[col(GW, C_FX_Q), col(GW, C_FX_K), col(GW, C_FX_V), col(LANES, C_FX_G),
                      pl.BlockSpec((1, LANES), const2), pl.BlockSpec((PREP_TILE, PREP_TILE), const2)],
            out_specs=out_specs, out_shape=out_shape,
            scratch_shapes=[pltpu.VMEM((8, LANES), F32)],
            compiler_params=_params(2), name="fox_prep",
        )(p3, p3, p3, p3, gate_bias, tri)
    return pl.pallas_call(
        _mlstm_prep_kernel, grid=grid,
        in_specs=[col(2 * GW, C_ML_Q), col(GW, C_ML_V), col(LANES, C_ML_G),
                  pl.BlockSpec((1, LANES), const2), pl.BlockSpec((8, 2 * GW), const2),
                  pl.BlockSpec((PREP_TILE, PREP_TILE), const2)],
        out_specs=out_specs, out_shape=out_shape,
        scratch_shapes=[pltpu.VMEM((8, LANES), F32), pltpu.VMEM((PREP_TILE + 8, 2 * GW), F32)],
        compiler_params=_params(2), name="mlstm_prep",
    )(p3, p3, p3, gate_bias, conv_w, tri)


def _causal_mask(tq):
    row = lax.broadcasted_iota(jnp.int32, (tq, tq), 0)
    colv = lax.broadcasted_iota(jnp.int32, (tq, tq), 1)
    return colv <= row


def _fox_flash_kernel(qa_ref, ka_ref, v_ref, out_ref):
    i = pl.program_id(2)
    tq = FLASH_TILE
    heads = range(FLASH_HEADS)
    qas = [qa_ref[h] for h in heads]

    def update(carry, s, v):
        m, l, acc = carry
        m_new = jnp.maximum(m, jnp.max(s, axis=1, keepdims=True))
        alpha = jnp.exp(m - m_new)
        p = jnp.exp(s - m_new)
        l = alpha * l + jnp.sum(p, axis=1, keepdims=True)
        acc = alpha * acc + _dot(_mx(p), v)
        return m_new, l, acc

    def body(j, carries):
        off = pl.multiple_of(j * tq, tq)
        ss = [_dot_nt(qas[h], ka_ref[h, pl.ds(off, tq), :]) for h in heads]
        return tuple(update(carries[h], ss[h], v_ref[h, pl.ds(off, tq), :]) for h in heads)

    init = tuple((jnp.full((tq, 1), NEG_BIG, F32), jnp.zeros((tq, 1), F32), jnp.zeros((tq, FX_DH), F32))
                 for _ in heads)
    carries = lax.fori_loop(0, i, body, init)
    off = pl.multiple_of(i * tq, tq)
    mask = _causal_mask(tq)
    outs = []
    for h in heads:
        s = jnp.where(mask, _dot_nt(qas[h], ka_ref[h, pl.ds(off, tq), :]), NEG_BIG)
        _, l, acc = update(carries[h], s, v_ref[h, pl.ds(off, tq), :])
        outs.append(_mx(acc / l))
    out_ref[...] = jnp.concatenate(outs, axis=1)


def _mlstm_flash_kernel(qa_ref, ka_ref, v_ref, o_ref, ng_ref, out_ref):
    i = pl.program_id(2)
    tq = FLASH_TILE
    heads = range(FLASH_HEADS)
    qas = [qa_ref[h] for h in heads]
    qs = [qa[:, :ML_DH] for qa in qas]
    qbs = [qa[:, ML_DH:] for qa in qas]

    def update(carry, qk, d, v):
        m, l, acc = carry
        m_new = jnp.maximum(m, jnp.max(d, axis=1, keepdims=True))
        alpha = jnp.exp(m - m_new)
        p = jnp.exp(d - m_new) * qk
        l = alpha * l + jnp.sum(p, axis=1, keepdims=True)
        acc = alpha * acc + _dot(_mx(p), v)
        return m_new, l, acc

    def tiles(h, off):
        ka = ka_ref[h, pl.ds(off, tq), :]
        return _dot_nt(qs[h], ka[:, :ML_DH]), _dot_nt(qbs[h], ka[:, ML_DH:]), v_ref[h, pl.ds(off, tq), :]

    def body(j, carries):
        off = pl.multiple_of(j * tq, tq)
        ts = [tiles(h, off) for h in heads]
        return tuple(update(carries[h], *ts[h]) for h in heads)

    lane = lax.broadcasted_iota(jnp.int32, (tq, LANES), 1)
    init = tuple((jnp.sum(jnp.where(lane < 3, qbs[h].astype(F32), 0.0), axis=1, keepdims=True),
                  jnp.zeros((tq, 1), F32), jnp.zeros((tq, ML_DH), F32)) for h in heads)
    carries = lax.fori_loop(0, i, body, init)
    off = pl.multiple_of(i * tq, tq)
    mask = _causal_mask(tq)
    ng = ng_ref[...]
    outs = []
    for h in heads:
        qk, d, v = tiles(h, off)
        m, l, acc = update(carries[h], qk, jnp.where(mask, d, NEG_BIG), v)
        hh = acc / jnp.maximum(jnp.abs(l), jnp.exp(-m))
        outs.append(hh * lax.rsqrt(jnp.mean(hh * hh, axis=1, keepdims=True) + NORM_EPS)
                    * ng[:, h * ML_DH:(h + 1) * ML_DH])
    out_ref[...] = _mx(_sigmoid(o_ref[...]) * jnp.concatenate(outs, axis=1))


def _flash(mode, qa, ka, vb, p3, norm_g):
    b, nh, s, _ = qa.shape
    tq, hp = FLASH_TILE, FLASH_HEADS
    w = hp * LANES
    grid = (b, nh // hp, s // tq)
    qspec = pl.BlockSpec((None, hp, tq, 2 * LANES), lambda bi, h, i: (bi, h, i, 0))
    kspec = pl.BlockSpec((None, hp, s, 2 * LANES), lambda bi, h, i: (bi, h, 0, 0))
    vspec = pl.BlockSpec((None, hp, s, LANES), lambda bi, h, i: (bi, h, 0, 0))
    ospec = pl.BlockSpec((None, tq, w), lambda bi, h, i: (bi, i, h))
    oshape = jax.ShapeDtypeStruct((b, s, GW), MXU_DTYPE)
    if mode == "fox":
        return pl.pallas_call(
            _fox_flash_kernel, grid=grid, in_specs=[qspec, kspec, vspec], out_specs=ospec, out_shape=oshape,
            compiler_params=_params(3), name="fox_flash",
        )(qa, ka, vb)
    gate = pl.BlockSpec((None, tq, w), lambda bi, h, i: (bi, i, C_ML_O // w + h))
    ng = pl.BlockSpec((1, w), lambda bi, h, i: (0, h))
    return pl.pallas_call(
        _mlstm_flash_kernel, grid=grid, in_specs=[qspec, kspec, vspec, gate, ng],
        out_specs=ospec, out_shape=oshape, compiler_params=_params(3), name="mlstm_flash",
    )(qa, ka, vb, p3, norm_g)


def _hgrn2_kernel(q_ref, z_ref, i_ref, g_ref, lb_ref, ng_ref, mats_ref, out_ref, state_ref):
    L = HG_CHUNK
    dh = GW // HG_HEADS

    @pl.when(pl.program_id(1) == 0)
    def _():
        state_ref[...] = jnp.zeros_like(state_ref)

    lb = lb_ref[...]
    z = z_ref[...]
    q = _silu(q_ref[...])
    v = _silu(i_ref[...])
    lf = _log_sigmoid(z) + jnp.log1p(lb * jnp.exp(jnp.minimum(-z, EXP_CLIP)))
    k = (1.0 - lb) * _sigmoid(-z)
    sums = _select_matmul(mats_ref[...], lf)
    b = sums[0:L]
    sfx = sums[L:2 * L]
    row = lax.broadcasted_iota(jnp.int32, (L, L), 0)
    colv = lax.broadcasted_iota(jnp.int32, (L, L), 1)
    diff = row ^ colv
    lower = row > colv
    sls = [slice(h * dh, (h + 1) * dh) for h in range(HG_HEADS)]
    att = [jnp.zeros((L, L), F32) for _ in sls]
    for lev in range(HG_LEVELS):
        dq = sums[(2 + lev) * L:(3 + lev) * L]
        dk = sums[(2 + HG_LEVELS + lev) * L:(3 + HG_LEVELS + lev) * L]
        ql = _mx(q * jnp.exp(dq))
        kl = _mx(k * jnp.exp(dk))
        mask = ((diff >> lev) == 1) & lower
        att = [jnp.where(mask, _dot_nt(ql[:, sl], kl[:, sl]), a) for sl, a in zip(sls, att)]
    vb = _mx(v)
    qe = _mx(q * jnp.exp(b))
    ke = _mx(k * jnp.exp(sfx))
    w_last = jnp.exp(b[L - 1:L, :])
    qk = q * k
    outs = []
    for h, sl in enumerate(sls):
        st = state_ref[h]
        o = _dot(_mx(att[h]), vb[:, sl]) + jnp.sum(qk[:, sl], axis=1, keepdims=True) * v[:, sl]
        o = o + _dot_nt(qe[:, sl], _mx(st))
        state_ref[h] = st * w_last[:, sl] + _dot_tn(vb[:, sl], ke[:, sl])
        outs.append(o)
    o = jnp.concatenate(outs, axis=1) * _sigmoid(g_ref[...])
    ng = ng_ref[...]
    outs = [o[:, sl] * lax.rsqrt(jnp.mean(o[:, sl] * o[:, sl], axis=1, keepdims=True) + NORM_EPS) * ng[:, sl]
            for sl in sls]
    out_ref[...] = _mx(jnp.concatenate(outs, axis=1))


def _hgrn2_mats():
    L = HG_CHUNK
    t = np.arange(L)[:, None]
    j = np.arange(L)[None, :]
    blocks = [j <= t, j > t]
    for lev in range(HG_LEVELS):
        start = (t >> lev) << lev
        blocks.append((j > start) & (j <= t))
    for lev in range(HG_LEVELS):
        nxt = ((t >> lev) + 1) << lev
        blocks.append((j > t) & (j <= nxt))
    return jnp.asarray(np.concatenate(blocks, axis=0).astype(np.float32), dtype=MXU_DTYPE)


def _hgrn2(p3, lb, norm_g):
    b, s, _ = p3.shape
    L = HG_CHUNK

    def col(offset):
        return pl.BlockSpec((None, L, GW), lambda bi, c: (bi, c, offset // GW))

    hvec = pl.BlockSpec((1, GW), lambda bi, c: (0, 0))
    mats = _hgrn2_mats()
    return pl.pallas_call(
        _hgrn2_kernel, grid=(b, s // L),
        in_specs=[col(C_HG_Q), col(C_HG_Z), col(C_HG_I), col(C_HG_G), hvec, hvec,
                  pl.BlockSpec(mats.shape, lambda bi, c: (0, 0))],
        out_specs=pl.BlockSpec((None, L, GW), lambda bi, c: (bi, c, 0)),
        out_shape=jax.ShapeDtypeStruct((b, s, GW), MXU_DTYPE),
        scratch_shapes=[pltpu.VMEM((HG_HEADS, LANES, LANES), F32)],
        compiler_params=_params(2), name="hgrn2",
    )(p3, p3, p3, p3, lb, norm_g, mats)


def _rwkv_kernel(rkv_ref, gd_ref, wa_ref, mu_rkv, mu_gd, mu_wa, w0_ref, a0_ref, kk_ref, ka_ref, rk_ref,
                 gng_ref, gnb_ref, w2_ref, a2_ref, g2_ref, tri_ref, out_ref,
                 sh_rkv, sh_gd, sh_wa, state_ref):
    L = RW_CHUNK
    first = pl.program_id(1) == 0

    @pl.when(first)
    def _():
        state_ref[...] = jnp.zeros_like(state_ref)

    def lerp(x_ref, sh_ref, mu_ref):
        x = x_ref[...]
        sh = _shift_rows(x, sh_ref, first)
        return x + (sh[7:7 + L, :] - x) * mu_ref[...]

    rkv = lerp(rkv_ref, sh_rkv, mu_rkv)
    gd = lerp(gd_ref, sh_gd, mu_gd)
    wa = lerp(wa_ref, sh_wa, mu_wa)
    r_all, k_all, v_all = rkv[:, :RW_W], rkv[:, RW_W:2 * RW_W], rkv[:, 2 * RW_W:]
    wd, ad = wa[:, :LANES], wa[:, LANES:]

    wpre = w0_ref[...] + _dot(_mx(jnp.tanh(wd)), w2_ref[...])
    wlog_all = -jnp.exp(-_softplus(-wpre) - 0.5)
    a_all = _sigmoid(a0_ref[...] + _dot(_mx(ad), a2_ref[...]))
    g_all = _dot(_mx(_sigmoid(gd)), g2_ref[...])

    tri = tri_ref[...]
    row = lax.broadcasted_iota(jnp.int32, (L, L), 0)
    colv = lax.broadcasted_iota(jnp.int32, (L, L), 1)
    diff = row ^ colv
    strict = row > colv
    causal = row >= colv
    eye = row == colv
    lane = lax.broadcasted_iota(jnp.int32, (L, RW_PAD), 1)
    valid = lane < RW_DH
    mid = L // 2

    heads = range(RW_HEADS)
    sls = [slice(h * RW_PAD, (h + 1) * RW_PAD) for h in heads]
    hd = []
    for sl in sls:
        r, k, v, a, wlog = r_all[:, sl], k_all[:, sl], v_all[:, sl], a_all[:, sl], wlog_all[:, sl]
        kk = k * kk_ref[:, sl]
        kk = kk * lax.rsqrt(jnp.maximum(jnp.sum(kk * kk, axis=1, keepdims=True), 1e-24))
        hd.append(dict(r=r, v=v, vb=_mx(v), kk=kk, beta=kk * a, kt=k * (1.0 + (a - 1.0) * ka_ref[:, sl]),
                       wlog=wlog))
    for d in hd:
        d["b"] = _select_matmul(tri, d["wlog"])
    for d in hd:
        b = d["b"]
        bp = b - d["wlog"]
        bm = b[mid:mid + 1, :]
        bl = b[L - 1:L, :]
        e_out = jnp.exp(bm - b)
        e_end = jnp.exp(bl - b)
        d["lhs"] = _mx(jnp.concatenate([d["kk"] * jnp.exp(bp - bm), d["r"] * jnp.exp(b - bm)], axis=0))
        d["rhs_b"] = _mx(d["beta"] * e_out)
        d["rhs_k"] = _mx(d["kt"] * e_out)
        d["kd"] = d["kk"] * jnp.exp(bp)
        d["rd"] = d["r"] * jnp.exp(b)
        d["kte"] = _mx(d["kt"] * e_end)
        d["be"] = _mx(d["beta"] * e_end)
        d["wl"] = jnp.exp(bl)
    for d in hd:
        pb = _dot_nt(d["lhs"], d["rhs_b"])
        pk = _dot_nt(d["lhs"], d["rhs_k"])
        d["amat"] = jnp.where(strict, pb[:L], 0.0)
        d["dmat"] = _mx(jnp.where(causal, pb[L:], 0.0))
        d["bmat"] = _mx(jnp.where(strict, pk[:L], 0.0))
        d["cmat"] = _mx(jnp.where(causal, pk[L:], 0.0))
        d["tinv"] = jnp.where(eye, 1.0, 0.0) - jnp.where(diff == 1, d["amat"], 0.0)
    for lev in range(1, RW_LEVELS):
        lev_mask = (diff >> lev) == 1
        for d in hd:
            d["ta"] = _dot(_mx(d["tinv"]), _mx(jnp.where(lev_mask, d["amat"], 0.0)))
        for d in hd:
            d["tinv"] = d["tinv"] - _dot(_mx(d["ta"]), _mx(d["tinv"]))
    for d in hd:
        d["bv"] = _dot(d["bmat"], d["vb"])
        d["cv"] = _dot(d["cmat"], d["vb"])
    for d in hd:
        d["uk"] = _dot(_mx(d["tinv"]), _mx(jnp.concatenate([d["bv"], d["kd"]], axis=1)))
    for d in hd:
        ukb = _mx(d["uk"])
        d["yr"] = jnp.concatenate([d["cv"], d["rd"]], axis=1) - _dot(d["dmat"], ukb)
        d["ub"], d["kb"] = ukb[:, :RW_PAD], ukb[:, RW_PAD:]
    for d in hd:
        d["gmat"] = jnp.where(eye, d["wl"], 0.0) - _dot_tn(d["kb"], d["be"])
        d["hmat"] = _dot_tn(d["vb"], d["kte"]) - _dot_tn(d["ub"], d["be"])
    for h, d in enumerate(hd):
        st = state_ref[h]
        d["y"] = _dot_nt(_mx(d["yr"][:, RW_PAD:]), _mx(st)) + d["yr"][:, :RW_PAD]
        state_ref[h] = _dot_hi(st, d["gmat"]) + d["hmat"]
    outs = []
    for sl, d in zip(sls, hd):
        y = d["y"]
        mu_y = jnp.sum(y, axis=1, keepdims=True) * (1.0 / RW_DH)
        dy = jnp.where(valid, y - mu_y, 0.0)
        var_y = jnp.sum(dy * dy, axis=1, keepdims=True) * (1.0 / RW_DH)
        yn = dy * lax.rsqrt(var_y + RW_GN_EPS) * gng_ref[:, sl] + gnb_ref[:, sl]
        bonus = jnp.sum(d["r"] * d["kt"] * rk_ref[:, sl], axis=1, keepdims=True) * d["v"]
        outs.append((yn + bonus) * g_all[:, sl])
    out_ref[...] = _mx(jnp.concatenate(outs, axis=1))


def _rwkv(p3, rp, tri):
    b, s, _ = p3.shape
    L = RW_CHUNK
    c2 = lambda bi, c: (0, 0)

    def vec(width):
        return pl.BlockSpec((1, width), c2)

    return pl.pallas_call(
        _rwkv_kernel, grid=(b, s // L),
        in_specs=[pl.BlockSpec((None, L, 3 * RW_W), lambda bi, c: (bi, c, 0)),
                  pl.BlockSpec((None, L, 2 * LANES), lambda bi, c: (bi, c, C_RW_GD // (2 * LANES))),
                  pl.BlockSpec((None, L, 2 * LANES), lambda bi, c: (bi, c, C_RW_WD // (2 * LANES))),
                  vec(3 * RW_W), vec(2 * LANES), vec(2 * LANES),
                  vec(RW_W), vec(RW_W), vec(RW_W), vec(RW_W), vec(RW_W), vec(RW_W), vec(RW_W),
                  pl.BlockSpec((LANES, RW_W), c2), pl.BlockSpec((LANES, RW_W), c2),
                  pl.BlockSpec((2 * LANES, RW_W), c2), pl.BlockSpec((L, L), c2)],
        out_specs=pl.BlockSpec((None, L, RW_W), lambda bi, c: (bi, c, 0)),
        out_shape=jax.ShapeDtypeStruct((b, s, RW_W), MXU_DTYPE),
        scratch_shapes=[pltpu.VMEM((L + 8, 3 * RW_W), F32), pltpu.VMEM((L + 8, 2 * LANES), F32),
                        pltpu.VMEM((L + 8, 2 * LANES), F32), pltpu.VMEM((RW_HEADS, RW_PAD, RW_PAD), F32)],
        compiler_params=_params(2), name="rwkv7",
    )(p3, p3, p3, rp["mu_rkv"], rp["mu_gd"], rp["mu_wa"], rp["w0"], rp["a0"], rp["k_k"], rp["k_a"], rp["r_k"],
      rp["gn_g"], rp["gn_b"], rp["w2"], rp["a2"], rp["g2"], tri)


def _outproj_kernel(yml_ref, yrw_ref, yfx_ref, yhg_ref, wml_ref, wrw_ref, wfx_ref, whg_ref,
                    x_ref, g_ref, b_ref, o_ref):
    mix = _dot(yml_ref[...], wml_ref[...]) + _dot(yrw_ref[...], wrw_ref[...])
    mix = mix + _dot(yfx_ref[...], wfx_ref[...]) + _dot(yhg_ref[...], whg_ref[...])
    o_ref[...] = _layer_norm(ALPHA * x_ref[...] + mix, g_ref[...], b_ref[...])


def _outproj(y_ml, y_rw, y_fx, y_hg, wo, x2d, g, bvec):
    t = x2d.shape[0]
    tm = ROW_TILE // 2
    row = lambda w: pl.BlockSpec((tm, w), lambda i: (i, 0))
    full = lambda a: pl.BlockSpec(a.shape, lambda i: (0, 0))
    return pl.pallas_call(
        _outproj_kernel, grid=(t // tm,),
        in_specs=[row(GW), row(RW_W), row(GW), row(GW), full(wo["ml"]), full(wo["rw"]), full(wo["fx"]),
                  full(wo["hg"]), row(D_MODEL), full(g), full(bvec)],
        out_specs=row(D_MODEL), out_shape=jax.ShapeDtypeStruct((t, D_MODEL), F32),
        compiler_params=_params(1), name="outproj_ln",
    )(y_ml, y_rw, y_fx, y_hg, wo["ml"], wo["rw"], wo["fx"], wo["hg"], x2d, g, bvec)


def _ffn_kernel(x_ref, wg_ref, wu_ref, wd_ref, g_ref, b_ref, o_ref, xb_ref, acc_ref):
    j = pl.program_id(1)

    @pl.when(j == 0)
    def _():
        xb_ref[...] = _mx(x_ref[...])
        acc_ref[...] = jnp.zeros_like(acc_ref)

    xb = xb_ref[...]
    h = _silu(_dot(xb, wg_ref[...])) * _dot(xb, wu_ref[...])
    acc_ref[...] += _dot(_mx(h), wd_ref[...])

    @pl.when(j == pl.num_programs(1) - 1)
    def _():
        o_ref[...] = _layer_norm(ALPHA * x_ref[...] + acc_ref[...], g_ref[...], b_ref[...])


def _ffn(x2d, wg, wu, wd, g, bvec):
    t = x2d.shape[0]
    tm, tf = ROW_TILE, FF_TILE
    vec = pl.BlockSpec((1, D_MODEL), lambda i, j: (0, 0))
    return pl.pallas_call(
        _ffn_kernel, grid=(t // tm, D_FF_PAD // tf),
        in_specs=[pl.BlockSpec((tm, D_MODEL), lambda i, j: (i, 0)),
                  pl.BlockSpec((D_MODEL, tf), lambda i, j: (0, j)),
                  pl.BlockSpec((D_MODEL, tf), lambda i, j: (0, j)),
                  pl.BlockSpec((tf, D_MODEL), lambda i, j: (j, 0)), vec, vec],
        out_specs=pl.BlockSpec((tm, D_MODEL), lambda i, j: (i, 0)),
        out_shape=jax.ShapeDtypeStruct((t, D_MODEL), F32),
        scratch_shapes=[pltpu.VMEM((tm, D_MODEL), MXU_DTYPE), pltpu.VMEM((tm, D_MODEL), F32)],
        compiler_params=_params(2), name="ffn_ln",
    )(x2d, wg, wu, wd, g, bvec)


def _router_kernel(x_ref, w_ref, idx_ref, wt_ref):
    logits = _dot_hi(x_ref[...], w_ref[...])
    lane = lax.broadcasted_iota(jnp.int32, logits.shape, 1)
    valid = lane < N_EXPERTS
    logits = jnp.where(valid, logits, NEG_BIG)
    e = jnp.exp(logits - jnp.max(logits, axis=1, keepdims=True))
    probs = jnp.where(valid, e / jnp.sum(e, axis=1, keepdims=True), -1.0)
    p1 = jnp.max(probs, axis=1, keepdims=True)
    i1 = jnp.min(jnp.where(probs == p1, lane, LANES), axis=1, keepdims=True)
    rest = jnp.where(lane == i1, -1.0, probs)
    p2 = jnp.max(rest, axis=1, keepdims=True)
    i2 = jnp.min(jnp.where(rest == p2, lane, LANES), axis=1, keepdims=True)
    tot = p1 + p2
    idx_ref[...] = jnp.where(lane == 0, i1, jnp.where(lane == 1, i2, 0))
    wt_ref[...] = jnp.where(lane == 0, p1 / tot, jnp.where(lane == 1, p2 / tot, 0.0))


def _router(x2d, w_router):
    t = x2d.shape[0]
    tm = ROW_TILE
    out = pl.BlockSpec((tm, LANES), lambda i: (i, 0))
    return pl.pallas_call(
        _router_kernel, grid=(t // tm,),
        in_specs=[pl.BlockSpec((tm, D_MODEL), lambda i: (i, 0)), pl.BlockSpec((D_MODEL, LANES), lambda i: (0, 0))],
        out_specs=[out, out],
        out_shape=[jax.ShapeDtypeStruct((t, LANES), jnp.int32), jax.ShapeDtypeStruct((t, LANES), F32)],
        compiler_params=_params(1), name="moe_router",
    )(x2d, w_router)


def _route(idx2, n_slots):
    t = idx2.shape[0]
    e_flat = idx2.reshape(-1)
    onehot = (e_flat[:, None] == jnp.arange(N_EXPERTS, dtype=jnp.int32)[None, :]).astype(jnp.int32)
    csum = jnp.cumsum(onehot, axis=0)
    rank = jnp.sum(onehot * csum, axis=1) - 1
    counts = csum[-1]
    tiles_per = (counts + ROW_TILE - 1) // ROW_TILE
    tile_end = jnp.cumsum(tiles_per)
    tile_start = tile_end - tiles_per
    pos = jnp.sum(onehot * tile_start[None, :], axis=1) * ROW_TILE + rank
    tok = jnp.zeros((n_slots,), jnp.int32).at[pos].set(jnp.arange(2 * t, dtype=jnp.int32) // 2)
    n_tiles = n_slots // ROW_TILE
    tile_ids = jnp.arange(n_tiles, dtype=jnp.int32)
    tile_expert = jnp.sum((tile_ids[:, None] >= tile_end[None, :]).astype(jnp.int32), axis=1)
    return tok, pos, jnp.minimum(tile_expert, N_EXPERTS - 1)


def _row_copy(src_hbm, src_row, dst_ref, dst_row, sem):
    return pltpu.make_async_copy(src_hbm.at[pl.ds(src_row, 1)], dst_ref.at[pl.ds(dst_row, 1)], sem)


def _expert_ffn_kernel(te_ref, tok_ref, x_hbm, wg_ref, wu_ref, wd_ref, o_ref, xg_ref, xb_ref, acc_ref, sem):
    j = pl.program_id(1)

    @pl.when(j == 0)
    def _():
        def issue(r, c):
            _row_copy(x_hbm, tok_ref[r], xg_ref, r, sem).start()
            return c

        def drain(r, c):
            _row_copy(x_hbm, tok_ref[r], xg_ref, r, sem).wait()
            return c

        lax.fori_loop(0, ROW_TILE, issue, 0)
        lax.fori_loop(0, ROW_TILE, drain, 0)
        xb_ref[...] = _mx(xg_ref[...])
        acc_ref[...] = jnp.zeros_like(acc_ref)

    xb = xb_ref[...]
    h = _silu(_dot(xb, wg_ref[...])) * _dot(xb, wu_ref[...])
    acc_ref[...] += _dot(_mx(h), wd_ref[...])

    @pl.when(j == pl.num_programs(1) - 1)
    def _():
        o_ref[...] = acc_ref[...]


def _expert_ffn(tile_expert, tok, x2d, wg, wu, wd):
    n = tok.shape[0]
    tm, tf = ROW_TILE, FF_TILE
    grid_spec = pltpu.PrefetchScalarGridSpec(
        num_scalar_prefetch=1, grid=(n // tm, D_FF_PAD // tf),
        in_specs=[pl.BlockSpec((tm,), lambda i, j, te: (i,), memory_space=pltpu.SMEM),
                  pl.BlockSpec(memory_space=pl.ANY),
                  pl.BlockSpec((None, D_MODEL, tf), lambda i, j, te: (te[i], 0, j)),
                  pl.BlockSpec((None, D_MODEL, tf), lambda i, j, te: (te[i], 0, j)),
                  pl.BlockSpec((None, tf, D_MODEL), lambda i, j, te: (te[i], j, 0))],
        out_specs=pl.BlockSpec((tm, D_MODEL), lambda i, j, te: (i, 0)),
        scratch_shapes=[pltpu.VMEM((tm, D_MODEL), F32), pltpu.VMEM((tm, D_MODEL), MXU_DTYPE),
                        pltpu.VMEM((tm, D_MODEL), F32), pltpu.SemaphoreType.DMA])
    return pl.pallas_call(
        _expert_ffn_kernel, grid_spec=grid_spec,
        out_shape=jax.ShapeDtypeStruct((n, D_MODEL), F32),
        compiler_params=_params(2), name="moe_expert_ffn",
    )(tile_expert, tok, x2d, wg, wu, wd)


def _combine_kernel(pos_ref, x_ref, wt_ref, g_ref, b_ref, ys_hbm, o_ref, buf_ref, sem):
    tm = COMBINE_ROWS

    def issue(r, c):
        _row_copy(ys_hbm, pos_ref[2 * r], buf_ref.at[0], r, sem).start()
        _row_copy(ys_hbm, pos_ref[2 * r + 1], buf_ref.at[1], r, sem).start()
        return c

    def drain(r, c):
        _row_copy(ys_hbm, pos_ref[2 * r], buf_ref.at[0], r, sem).wait()
        _row_copy(ys_hbm, pos_ref[2 * r + 1], buf_ref.at[1], r, sem).wait()
        return c

    lax.fori_loop(0, tm, issue, 0)
    lax.fori_loop(0, tm, drain, 0)
    wt = wt_ref[...]
    ff = _lane_pick(wt, 0) * buf_ref[0] + _lane_pick(wt, 1) * buf_ref[1]
    o_ref[...] = _layer_norm(ALPHA * x_ref[...] + ff, g_ref[...], b_ref[...])


def _moe_combine(pos, ys, x2d, wts, g, bvec):
    t = x2d.shape[0]
    tm = COMBINE_ROWS
    vec = pl.BlockSpec((1, D_MODEL), lambda i: (0, 0))
    return pl.pallas_call(
        _combine_kernel, grid=(t // tm,),
        in_specs=[pl.BlockSpec((2 * tm,), lambda i: (i,), memory_space=pltpu.SMEM),
                  pl.BlockSpec((tm, D_MODEL), lambda i: (i, 0)),
                  pl.BlockSpec((tm, LANES), lambda i: (i, 0)), vec, vec,
                  pl.BlockSpec(memory_space=pl.ANY)],
        out_specs=pl.BlockSpec((tm, D_MODEL), lambda i: (i, 0)),
        out_shape=jax.ShapeDtypeStruct((t, D_MODEL), F32),
        scratch_shapes=[pltpu.VMEM((2, tm, D_MODEL), F32), pltpu.SemaphoreType.DMA],
        compiler_params=_params(1), name="moe_combine_ln",
    )(pos, x2d, wts, g, bvec, ys)


def _pad_cols(a, width):
    return jnp.pad(a, ((0, 0), (0, width - a.shape[1])))


def _pad_heads(a):
    lead = a.shape[:-1]
    a = a.reshape(lead + (RW_HEADS, RW_DH))
    a = jnp.pad(a, [(0, 0)] * len(lead) + [(0, 0), (0, RW_PAD - RW_DH)])
    return a.reshape(lead + (RW_W,))


def _pack_table():
    ml_n = 4 * GW + 2 * ML_HEADS
    rw_n = 3 * GW + RW_DECAY_LORA + RW_A_LORA + RW_GATE_LORA
    fx_n = 3 * GW + FX_HEADS
    rw0, fx0, hg0 = ml_n, ml_n + rw_n, ml_n + rw_n + fx_n
    start = np.zeros((N_PACK // LANES,), np.int32)
    nvalid = np.zeros((N_PACK // LANES,), np.int32)

    def put(dst, src, width):
        for off in range(0, width, LANES):
            start[(dst + off) // LANES] = src + off
            nvalid[(dst + off) // LANES] = min(LANES, width - off)

    for s, dst in enumerate((C_RW_R, C_RW_K, C_RW_V)):
        for h in range(RW_HEADS):
            put(dst + h * RW_PAD, rw0 + s * GW + h * RW_DH, RW_DH)
    put(C_ML_Q, 0, 4 * GW)
    put(C_FX_Q, fx0, 3 * GW)
    put(C_HG_Q, hg0, 4 * GW)
    put(C_RW_GD, rw0 + 3 * GW + RW_DECAY_LORA + RW_A_LORA, RW_GATE_LORA)
    put(C_ML_G, 4 * GW, 2 * ML_HEADS)
    put(C_FX_G, fx0 + 3 * GW, FX_HEADS)
    put(C_RW_WD, rw0 + 3 * GW, RW_DECAY_LORA)
    put(C_RW_AD, rw0 + 3 * GW + RW_DECAY_LORA, RW_A_LORA)
    return start, nvalid


def _pack_kernel(blk_ref, shift_ref, nvalid_ref, wa_ref, wb_ref, o_ref):
    j = pl.program_id(0)
    gcol = lax.broadcasted_iota(jnp.int32, (1, 2 * LANES), 1) + blk_ref[j] * LANES
    cat = jnp.concatenate([wa_ref[...], wb_ref[...]], axis=1)
    cat = jnp.where(gcol < N_IN, cat, 0.0)
    r = lax.broadcasted_iota(jnp.int32, (2 * LANES, LANES), 0)
    c = lax.broadcasted_iota(jnp.int32, (2 * LANES, LANES), 1)
    sel = jnp.where((r == c + shift_ref[j]) & (c < nvalid_ref[j]), 1.0, 0.0)
    o_ref[...] = _mx(_dot(_mx(cat), _mx(sel)))


def _pack_w_in(w_in, layer):
    start, nvalid = _pack_table()
    last_blk = (N_IN - 1) // LANES
    blk = jnp.asarray(start // LANES)
    grid_spec = pltpu.PrefetchScalarGridSpec(
        num_scalar_prefetch=3, grid=(N_PACK // LANES,),
        in_specs=[pl.BlockSpec((None, D_MODEL, LANES), lambda j, b, s, n: (layer, 0, b[j])),
                  pl.BlockSpec((None, D_MODEL, LANES), lambda j, b, s, n: (layer, 0, jnp.minimum(b[j] + 1, last_blk)))],
        out_specs=pl.BlockSpec((D_MODEL, LANES), lambda j, b, s, n: (0, j)))
    return pl.pallas_call(
        _pack_kernel, grid_spec=grid_spec,
        out_shape=jax.ShapeDtypeStruct((D_MODEL, N_PACK), MXU_DTYPE),
        compiler_params=_params(1), name="pack_w_in",
    )(blk, jnp.asarray(start % LANES), jnp.asarray(nvalid), w_in, w_in)


def _row(a, width=None):
    a = a.reshape(1, -1).astype(F32)
    return a if width is None else _pad_cols(a, width)


def _pad_rows(a, rows):
    return jnp.pad(a, ((0, rows - a.shape[0]), (0, 0)))


def _rwkv_params(mu, w0, w2, a0, a2, g2, k_k, k_a, r_k, gn_g, gn_b):
    o = 3 * GW
    mu_rkv = jnp.concatenate([_pad_heads(mu[:GW]), _pad_heads(mu[GW:2 * GW]), _pad_heads(mu[2 * GW:o])])
    mu_wd = jnp.pad(mu[o:o + RW_DECAY_LORA], (0, LANES - RW_DECAY_LORA))
    mu_ad = jnp.pad(mu[o + RW_DECAY_LORA:o + RW_DECAY_LORA + RW_A_LORA], (0, LANES - RW_A_LORA))
    return {
        "mu_rkv": _row(mu_rkv), "mu_gd": _row(mu[o + RW_DECAY_LORA + RW_A_LORA:]),
        "mu_wa": _row(jnp.concatenate([mu_wd, mu_ad])),
        "w0": _row(_pad_heads(w0)), "a0": _row(_pad_heads(a0)), "k_k": _row(_pad_heads(k_k)),
        "k_a": _row(_pad_heads(k_a)), "r_k": _row(_pad_heads(r_k.reshape(-1))),
        "gn_g": _row(_pad_heads(gn_g)), "gn_b": _row(_pad_heads(gn_b)),
        "w2": _pad_rows(_pad_heads(w2), LANES).astype(MXU_DTYPE),
        "a2": _pad_rows(_pad_heads(a2), LANES).astype(MXU_DTYPE),
        "g2": _pad_heads(g2).astype(MXU_DTYPE),
    }


def _pack_w_out(w):
    rw = w[GW:2 * GW].reshape(RW_HEADS, RW_DH, D_MODEL)
    rw = jnp.pad(rw, ((0, 0), (0, RW_PAD - RW_DH), (0, 0))).reshape(RW_W, D_MODEL)
    return {"ml": w[:GW].astype(MXU_DTYPE), "rw": rw.astype(MXU_DTYPE),
            "fx": w[2 * GW:3 * GW].astype(MXU_DTYPE), "hg": w[3 * GW:].astype(MXU_DTYPE)}


def _pad_ff(wg, wu, wd):
    padc = [(0, 0)] * (wg.ndim - 1) + [(0, D_FF_PAD - D_FF)]
    padr = [(0, 0)] * (wd.ndim - 2) + [(0, D_FF_PAD - D_FF), (0, 0)]
    return (jnp.pad(wg, padc).astype(MXU_DTYPE), jnp.pad(wu, padc).astype(MXU_DTYPE),
            jnp.pad(wd, padr).astype(MXU_DTYPE))


def _tri(n):
    return jnp.asarray(np.tril(np.ones((n, n), np.float32)), dtype=MXU_DTYPE)


def kernel(x, w_in, w_out, ln1_g, ln1_b, ln2_g, ln2_b, ml_conv, ml_b_i, ml_b_f, ml_norm_g, rw_mu, rw_w0, rw_w2, rw_a0, rw_a2, rw_g2, rw_k_k, rw_k_a, rw_r_k, rw_gn_g, rw_gn_b, fx_b_f, hg_lb_logits, hg_norm_g, ffn_w_gate, ffn_w_up, ffn_w_down, moe_router, moe_w_gate, moe_w_up, moe_w_down):
    bsz, seq, d = x.shape
    depth = w_in.shape[0]
    assert d == D_MODEL and seq % FLASH_TILE == 0 and (bsz * seq) % PROJ_ROW_TILE == 0
    lb_probs = jax.nn.softmax(hg_lb_logits.astype(F32), axis=0)
    lower_bounds = jnp.cumsum(lb_probs, axis=0) - lb_probs[0]
    tri_prep = _tri(PREP_TILE)
    tri_rw = _tri(RW_CHUNK)
    x2d = x.reshape(bsz * seq, d)
    for l in range(depth):
        p3 = _inproj(x2d, _pack_w_in(w_in, l)).reshape(bsz, seq, N_PACK)

        ml_gate_bias = _row(jnp.concatenate([ml_b_i[l], ml_b_f[l]]), LANES)
        qa, ka, vb = _flash_prep("mlstm", p3, ml_gate_bias, _pad_rows(ml_conv[l], 8), tri_prep)
        y_ml = _flash("mlstm", qa, ka, vb, p3, _row(ml_norm_g[l]))

        qa, ka, vb = _flash_prep("fox", p3, _row(fx_b_f[l], LANES), None, tri_prep)
        y_fx = _flash("fox", qa, ka, vb, None, None)

        y_hg = _hgrn2(p3, _row(lower_bounds[l]), _row(hg_norm_g[l]))

        rp = _rwkv_params(rw_mu[l], rw_w0[l], rw_w2[l], rw_a0[l], rw_a2[l], rw_g2[l], rw_k_k[l], rw_k_a[l],
                          rw_r_k[l], rw_gn_g[l], rw_gn_b[l])
        y_rw = _rwkv(p3, rp, tri_rw)

        t = bsz * seq
        x2d = _outproj(y_ml.reshape(t, GW), y_rw.reshape(t, RW_W), y_fx.reshape(t, GW), y_hg.reshape(t, GW),
                       _pack_w_out(w_out[l]), x2d, _row(ln1_g[l]), _row(ln1_b[l]))
        j = l // 2
        if l % 2 == 0:
            wg, wu, wd = _pad_ff(ffn_w_gate[j], ffn_w_up[j], ffn_w_down[j])
            x2d = _ffn(x2d, wg, wu, wd, _row(ln2_g[l]), _row(ln2_b[l]))
        else:
            idx, wts = _router(x2d, _pad_cols(moe_router[j].astype(F32), LANES))
            n_slots = 2 * t + N_EXPERTS * ROW_TILE
            tok, pos, tile_expert = _route(idx[:, :2], n_slots)
            wg, wu, wd = _pad_ff(moe_w_gate[j], moe_w_up[j], moe_w_down[j])
            ys = _expert_ffn(tile_expert, tok, x2d, wg, wu, wd)
            x2d = _moe_combine(pos, ys, x2d, wts, _row(ln2_g[l]), _row(ln2_b[l]))
    return x2d.reshape(bsz, seq, d)
```
